```python
import math
import jax, jax.numpy as jnp
from jax import lax
import numpy as np

D_MODEL = 1024
BATCH = 16
SEQ = 4096
DEPTH = 1
DEC_BATCH = 8
DEC_SEQ = 16
PAST_LEN = 1024

CHUNK = 64
N_HEADS = 8
QK_NOPE = 128
QK_ROPE = 64
QK_HEAD = QK_NOPE + QK_ROPE
V_HEAD = 128
Q_LORA = 512
KV_LORA = 512
ROPE_THETA = 10000.0
SCALE = QK_HEAD ** -0.5
Q_BLOCK = 128
NEG_INF = -1e30
D_RNN = D_MODEL
RNN_BLOCKS = 8
RNN_BLOCK = D_RNN // RNN_BLOCKS
CONV_W = 4
LRU_C = 8.0
D_FF = -(-8 * D_MODEL // (3 * 256)) * 256
EPS = 1e-6
OFF_Q = 0
OFF_KV = OFF_Q + Q_LORA
OFF_KR = OFF_KV + KV_LORA
OFF_X = OFF_KR + QK_ROPE
OFF_GA = OFF_X + D_RNN
OFF_GB = OFF_GA + D_MODEL
IN_TOTAL = OFF_GB + D_MODEL

kernel_name = 'hybrid_mla_rglru_stream_step'


def rms_norm(x, g):
    xf = x.astype(jnp.float32)
    y = xf * lax.rsqrt(jnp.mean(xf * xf, axis=-1, keepdims=True) + EPS)
    return (y * g.astype(jnp.float32)).astype(x.dtype)


def apply_rope(x, pos):
    half = QK_ROPE // 2
    inv_freq = jnp.exp(-math.log(ROPE_THETA) * jnp.arange(half, dtype=jnp.float32) / half)
    ang = pos.astype(jnp.float32)[:, None] * inv_freq[None, :]
    cos = jnp.cos(ang)[None, :, None, :]
    sin = jnp.sin(ang)[None, :, None, :]
    xf = x.astype(jnp.float32)
    x1, x2 = xf[..., :half], xf[..., half:]
    return jnp.concatenate([x1 * cos - x2 * sin, x1 * sin + x2 * cos], axis=-1).astype(x.dtype)


def mla_queries(c_q, pos, q_norm_g, w_uq, qk_q_g):
    b, t, _ = c_q.shape
    q = (rms_norm(c_q, q_norm_g) @ w_uq).reshape(b, t, N_HEADS, QK_HEAD)
    q = rms_norm(q, qk_q_g)
    return jnp.concatenate([q[..., :QK_NOPE], apply_rope(q[..., QK_NOPE:], pos)], axis=-1)


def mla_keys_values(c_kv, k_rope_raw, pos, w_ukv, qk_k_g):
    b, t, _ = c_kv.shape
    kv = (c_kv @ w_ukv).reshape(b, t, N_HEADS, QK_NOPE + V_HEAD)
    k_rope = jnp.broadcast_to(k_rope_raw[:, :, None, :], (b, t, N_HEADS, QK_ROPE))
    k = rms_norm(jnp.concatenate([kv[..., :QK_NOPE], k_rope], axis=-1), qk_k_g)
    k = jnp.concatenate([k[..., :QK_NOPE], apply_rope(k[..., QK_NOPE:], pos)], axis=-1)
    return k, kv[..., QK_NOPE:]


def chunk_causal_attend(q, q_pos, k, v, k_pos):
    s = jnp.einsum('bqhd,bkhd->bhqk', q, k, preferred_element_type=jnp.float32) * SCALE
    allowed = (k_pos[None, :] // CHUNK) <= (q_pos[:, None] // CHUNK)
    s = jnp.where(allowed[None, None], s, NEG_INF)
    p = jax.nn.softmax(s, axis=-1).astype(v.dtype)
    return jnp.einsum('bhqk,bkhd->bqhd', p, v)


def blocked_prompt_attention(q, k, v, pos):
    b, t, h, dk = q.shape
    nb = t // Q_BLOCK
    q_blocks = q.reshape(b, nb, Q_BLOCK, h, dk).transpose(1, 0, 2, 3, 4)
    pos_blocks = pos.reshape(nb, Q_BLOCK)
    out = lax.map(lambda qp: chunk_causal_attend(qp[0], qp[1], k, v, pos), (q_blocks, pos_blocks))
    return out.transpose(1, 0, 2, 3, 4).reshape(b, t, h, V_HEAD)


def causal_depthwise_conv(x, hist, w, bias):
    t = x.shape[1]
    xp = jnp.concatenate([hist, x], axis=1)
    y = bias + xp[:, 0:t] * w[0]
    for j in range(1, CONV_W):
        y = y + xp[:, j:j + t] * w[j]
    return y, xp[:, xp.shape[1] - (CONV_W - 1):]


def rg_lru(xc, h0, w_a, b_a, w_x, b_x, lam):
    b, t, c = xc.shape
    xb = xc.reshape(b, t, RNN_BLOCKS, RNN_BLOCK)
    r = jax.nn.sigmoid((jnp.einsum('btni,nij->btnj', xb, w_a).reshape(b, t, c) + b_a).astype(jnp.float32))
    i = jax.nn.sigmoid((jnp.einsum('btni,nij->btnj', xb, w_x).reshape(b, t, c) + b_x).astype(jnp.float32))
    log_a = -LRU_C * r * jax.nn.softplus(-lam.astype(jnp.float32))
    a = jnp.exp(log_a)
    u = jnp.sqrt(-jnp.expm1(2.0 * log_a)) * (i * xc.astype(jnp.float32))
    u = u.at[:, 0].add(a[:, 0] * h0.astype(jnp.float32))

    def combine(left, right):
        a_l, u_l = left
        a_r, u_r = right
        return a_l * a_r, a_r * u_l + u_r

    _, h = lax.associative_scan(combine, (a, u), axis=1)
    return h.astype(xc.dtype), h[:, -1].astype(xc.dtype)


def hybrid_layer(x, pos, past, p):
    b, t, _ = x.shape
    xn = rms_norm(x, p['norm1_g'])
    z = xn @ p['w_in']
    c_q = z[..., OFF_Q:OFF_KV]
    c_kv = rms_norm(z[..., OFF_KV:OFF_KR], p['kv_norm_g'])
    k_rope_raw = z[..., OFF_KR:OFF_X]
    x_rnn = z[..., OFF_X:OFF_GA]
    g_att = z[..., OFF_GA:OFF_GB]
    g_rnn = z[..., OFF_GB:IN_TOTAL]

    q = mla_queries(c_q, pos, p['q_norm_g'], p['w_uq'], p['qk_q_g'])
    if past is None:
        k, v = mla_keys_values(c_kv, k_rope_raw, pos, p['w_ukv'], p['qk_k_g'])
        att = blocked_prompt_attention(q, k, v, pos)
        conv_hist = jnp.zeros((b, CONV_W - 1, D_RNN), x.dtype)
        h0 = jnp.zeros((b, D_RNN), x.dtype)
    else:
        past_ckv, past_krope, past_pos, conv_hist, h0 = past
        k_pos = jnp.concatenate([past_pos, pos])
        k, v = mla_keys_values(jnp.concatenate([past_ckv, c_kv], axis=1),
                               jnp.concatenate([past_krope, k_rope_raw], axis=1),
                               k_pos, p['w_ukv'], p['qk_k_g'])
        att = chunk_causal_attend(q, pos, k, v, k_pos)

    xc, conv_new = causal_depthwise_conv(x_rnn, conv_hist, p['conv_w'], p['conv_b'])
    h_seq, h_last = rg_lru(xc, h0, p['w_rg_a'], p['b_rg_a'], p['w_rg_x'], p['b_rg_x'], p['lru_lambda'])

    o_att = att.reshape(b, t, N_HEADS * V_HEAD) @ p['w_proj_attn']
    o_rnn = h_seq @ p['w_proj_rnn']
    merged = jax.nn.sigmoid(g_att) * o_att + jax.nn.sigmoid(g_rnn) * o_rnn
    x = x + merged @ p['w_out']

    xn2 = rms_norm(x, p['norm2_g'])
    x = x + (jax.nn.silu(xn2 @ p['w_ffn_gate']) * (xn2 @ p['w_ffn_up'])) @ p['w_ffn_down']
    return x, c_kv, k_rope_raw, conv_new, h_last


def setup_inputs(seed: int = 0) -> dict:
    key = jax.random.key(seed)
    ks = jax.random.split(key, 32)

    def nrm(k, shape, scale):
        return jax.random.normal(k, shape, jnp.float32) * scale

    u = jax.random.uniform(ks[20], (DEPTH, D_RNN), jnp.float32, minval=0.9, maxval=0.999)
    a_base = u ** (1.0 / LRU_C)
    lru_lambda = jnp.log(a_base) - jnp.log1p(-a_base)
    return {
        'x_prompt': nrm(ks[0], (BATCH, SEQ, D_MODEL), 1.0),
        'x_sample': nrm(ks[1], (DEC_BATCH, DEC_SEQ, D_MODEL), 1.0),
        'cache_ckv': nrm(ks[2], (DEPTH, DEC_BATCH, PAST_LEN, KV_LORA), 1.0),
        'cache_krope': nrm(ks[3], (DEPTH, DEC_BATCH, PAST_LEN, QK_ROPE), 1.0),
        'state_conv': nrm(ks[4], (DEPTH, DEC_BATCH, CONV_W - 1, D_RNN), 1.0),
        'state_h': nrm(ks[5], (DEPTH, DEC_BATCH, D_RNN), 0.5),
        'norm1_g': 1.0 + nrm(ks[6], (DEPTH, D_MODEL), 0.02),
        'w_in': nrm(ks[7], (DEPTH, D_MODEL, IN_TOTAL), D_MODEL ** -0.5),
        'q_norm_g': 1.0 + nrm(ks[8], (DEPTH, Q_LORA), 0.02),
        'w_uq': nrm(ks[9], (DEPTH, Q_LORA, N_HEADS * QK_HEAD), Q_LORA ** -0.5),
        'kv_norm_g': 1.0 + nrm(ks[10], (DEPTH, KV_LORA), 0.02),
        'w_ukv': nrm(ks[11], (DEPTH, KV_LORA, N_HEADS * (QK_NOPE + V_HEAD)), KV_LORA ** -0.5),
        'qk_q_g': 1.0 + nrm(ks[12], (DEPTH, QK_HEAD), 0.02),
        'qk_k_g': 1.0 + nrm(ks[13], (DEPTH, QK_HEAD), 0.02),
        'conv_w': nrm(ks[14], (DEPTH, CONV_W, D_RNN), CONV_W ** -0.5),
        'conv_b': nrm(ks[15], (DEPTH, D_RNN), 0.02),
        'w_rg_a': nrm(ks[16], (DEPTH, RNN_BLOCKS, RNN_BLOCK, RNN_BLOCK), RNN_BLOCK ** -0.5),
        'b_rg_a': nrm(ks[17], (DEPTH, D_RNN), 0.02),
        'w_rg_x': nrm(ks[18], (DEPTH, RNN_BLOCKS, RNN_BLOCK, RNN_BLOCK), RNN_BLOCK ** -0.5),
        'b_rg_x': nrm(ks[19], (DEPTH, D_RNN), 0.02),
        'lru_lambda': lru_lambda,
        'w_proj_attn': nrm(ks[21], (DEPTH, N_HEADS * V_HEAD, D_MODEL), (N_HEADS * V_HEAD) ** -0.5),
        'w_proj_rnn': nrm(ks[22], (DEPTH, D_RNN, D_MODEL), D_RNN ** -0.5),
        'w_out': nrm(ks[23], (DEPTH, D_MODEL, D_MODEL), D_MODEL ** -0.5),
        'norm2_g': 1.0 + nrm(ks[24], (DEPTH, D_MODEL), 0.02),
        'w_ffn_gate': nrm(ks[25], (DEPTH, D_MODEL, D_FF), D_MODEL ** -0.5),
        'w_ffn_up': nrm(ks[26], (DEPTH, D_MODEL, D_FF), D_MODEL ** -0.5),
        'w_ffn_down': nrm(ks[27], (DEPTH, D_FF, D_MODEL), D_FF ** -0.5),
    }


def reference(x_prompt, x_sample, cache_ckv, cache_krope, state_conv, state_h,
              norm1_g, w_in, q_norm_g, w_uq, kv_norm_g, w_ukv, qk_q_g, qk_k_g,
              conv_w, conv_b, w_rg_a, b_rg_a, w_rg_x, b_rg_x, lru_lambda,
              w_proj_attn, w_proj_rnn, w_out, norm2_g, w_ffn_gate, w_ffn_up, w_ffn_down):
    past_len = cache_ckv.shape[2]
    pos_prompt = jnp.arange(x_prompt.shape[1], dtype=jnp.int32)
    past_pos = jnp.arange(past_len, dtype=jnp.int32)
    pos_sample = past_len + jnp.arange(x_sample.shape[1], dtype=jnp.int32)

    y_p, y_s = x_prompt, x_sample
    ckv_p, kr_p, conv_p, h_p = [], [], [], []
    ckv_s, kr_s, conv_s, h_s = [], [], [], []
    for l in range(DEPTH):
        p = dict(norm1_g=norm1_g[l], w_in=w_in[l], q_norm_g=q_norm_g[l], w_uq=w_uq[l],
                 kv_norm_g=kv_norm_g[l], w_ukv=w_ukv[l], qk_q_g=qk_q_g[l], qk_k_g=qk_k_g[l],
                 conv_w=conv_w[l], conv_b=conv_b[l], w_rg_a=w_rg_a[l], b_rg_a=b_rg_a[l],
                 w_rg_x=w_rg_x[l], b_rg_x=b_rg_x[l], lru_lambda=lru_lambda[l],
                 w_proj_attn=w_proj_attn[l], w_proj_rnn=w_proj_rnn[l], w_out=w_out[l],
                 norm2_g=norm2_g[l], w_ffn_gate=w_ffn_gate[l], w_ffn_up=w_ffn_up[l],
                 w_ffn_down=w_ffn_down[l])
        y_p, c1, k1, v1, h1 = hybrid_layer(y_p, pos_prompt, None, p)
        past = (cache_ckv[l], cache_krope[l], past_pos, state_conv[l], state_h[l])
        y_s, c2, k2, v2, h2 = hybrid_layer(y_s, pos_sample, past, p)
        ckv_p.append(c1); kr_p.append(k1); conv_p.append(v1); h_p.append(h1)
        ckv_s.append(c2); kr_s.append(k2); conv_s.append(v2); h_s.append(h2)

    new_ckv_prompt = jnp.stack(ckv_p)
    new_krope_prompt = jnp.stack(kr_p)
    new_conv_prompt = jnp.stack(conv_p)
    new_h_prompt = jnp.stack(h_p)
    new_ckv_sample = jnp.stack(ckv_s)
    new_krope_sample = jnp.stack(kr_s)
    new_conv_sample = jnp.stack(conv_s)
    new_h_sample = jnp.stack(h_s)
    return (y_p, y_s, new_ckv_prompt, new_krope_prompt, new_conv_prompt, new_h_prompt,
            new_ckv_sample, new_krope_sample, new_conv_sample, new_h_sample)
```

```python
import functools
import math

import jax
import jax.numpy as jnp
from jax import lax
from jax.experimental import pallas as pl
from jax.experimental.pallas import tpu as pltpu

CHUNK = 64
N_HEADS = 8
QK_NOPE = 128
QK_ROPE = 64
QK_HEAD = QK_NOPE + QK_ROPE
V_HEAD = 128
ROPE_THETA = 10000.0
CONV_W = 4
LRU_C = 8.0
RNN_BLOCKS = 8
EPS = 1e-6
NEG_INF = -1e30
SCALE = QK_HEAD ** -0.5
LOG2E = 1.4426950408889634

LANES = 128
SUBLANES = 8
BF16_ROWS = 16
VMEM_LIMIT_BYTES = 58 * 1024 * 1024

HEAD_PAD = 2 * LANES
HALF = QK_ROPE // 2
CHUNK_SHIFT = CHUNK.bit_length() - 1
assert 1 << CHUNK_SHIFT == CHUNK

F32 = jnp.float32
BF16 = jnp.bfloat16


def _tile(n, target):
  if n <= target:
    return n
  for t in range(target - target % BF16_ROWS, 0, -BF16_ROWS):
    if n % t == 0:
      return t
  return n


def _const_spec(shape):
  nd = len(shape)
  return pl.BlockSpec(shape, lambda *_: (0,) * nd, pipeline_mode=pl.Buffered(1))


def _params(*sem):
  return pltpu.CompilerParams(dimension_semantics=sem, vmem_limit_bytes=VMEM_LIMIT_BYTES)


def _rms(x, g):
  return x * lax.rsqrt(jnp.mean(x * x, axis=-1, keepdims=True) + EPS) * g


def _sigmoid(x):
  return 1.0 / (1.0 + jnp.exp(-x))


def _mm(a, b):
  return jnp.dot(a, b, preferred_element_type=F32)


def _in_proj_kernel(x_ref, g1_ref, w_ref, gq_ref, gkv_ref,
                    cqn_ref, ckv_ref, kr4_ref, kr_ref, xr_ref, xtail_ref, ga_ref, gb_ref,
                    *, widths):
  q_lora, kv_lora, d_rnn, d_model = widths
  xn = _rms(x_ref[0], g1_ref[...]).astype(BF16)
  off = 0

  def seg(n):
    nonlocal off
    r = _mm(xn, w_ref[:, off:off + n])
    off += n
    return r

  cqn_ref[0] = _rms(seg(q_lora), gq_ref[...]).astype(BF16)
  ckv_ref[0] = _rms(seg(kv_lora), gkv_ref[...])
  kr4 = seg(LANES)
  kr4_ref[0] = kr4
  kr_ref[0] = kr4[:, :QK_ROPE]
  xr = seg(d_rnn)
  xr_ref[0] = xr.astype(BF16)
  xtail_ref[0] = xr[xr.shape[0] - SUBLANES:, :]
  ga_ref[0] = _sigmoid(seg(d_model)).astype(BF16)
  gb_ref[0] = _sigmoid(seg(d_model)).astype(BF16)


def _in_proj(x, g1, w, gq, gkv, widths, tm):
  b, t, d = x.shape
  q_lora, kv_lora, d_rnn, d_model = widths
  row = lambda n: pl.BlockSpec((1, tm, n), lambda i, j: (i, j, 0))
  out_shape = (
      jax.ShapeDtypeStruct((b, t, q_lora), BF16),
      jax.ShapeDtypeStruct((b, t, kv_lora), F32),
      jax.ShapeDtypeStruct((b, t, LANES), F32),
      jax.ShapeDtypeStruct((b, t, QK_ROPE), F32),
      jax.ShapeDtypeStruct((b, t, d_rnn), BF16),
      jax.ShapeDtypeStruct((b, SUBLANES, d_rnn), F32),
      jax.ShapeDtypeStruct((b, t, d_model), BF16),
      jax.ShapeDtypeStruct((b, t, d_model), BF16),
  )
  out_specs = (row(q_lora), row(kv_lora), row(LANES), row(QK_ROPE), row(d_rnn),
               pl.BlockSpec((1, SUBLANES, d_rnn), lambda i, j: (i, 0, 0)),
               row(d_model), row(d_model))
  return pl.pallas_call(
      functools.partial(_in_proj_kernel, widths=widths),
      grid=(b, t // tm),
      in_specs=[row(d), _const_spec(g1.shape), _const_spec(w.shape),
                _const_spec(gq.shape), _const_spec(gkv.shape)],
      out_specs=out_specs,
      out_shape=out_shape,
      compiler_params=_params("parallel", "arbitrary"),
      name="in_proj",
  )(x, g1, w, gq, gkv)


def _rope4(t4, cs, g4):
  t = t4 * cs * g4
  r = t + pltpu.roll(t, 2 * HALF, axis=1)
  lane = lax.broadcasted_iota(jnp.int32, r.shape, 1)
  return jnp.where(lane < QK_ROPE, r, 0.0)


def _q_up_kernel(c_ref, w_ref, g_ref, cs_ref, q_ref):
  c = c_ref[0]
  cs = cs_ref[...]
  g = g_ref[...]
  for h in range(N_HEADS):
    y = _mm(c, w_ref[:, h * HEAD_PAD:(h + 1) * HEAD_PAD])
    yn, yr = y[:, :LANES], y[:, LANES:]
    ss = jnp.sum(yn * yn, axis=-1, keepdims=True) + 0.5 * jnp.sum(yr * yr, axis=-1, keepdims=True)
    inv = lax.rsqrt(ss * (1.0 / QK_HEAD) + EPS) * (SCALE * LOG2E)
    q_ref[0, h, :, :LANES] = (yn * inv * g[:, :LANES]).astype(BF16)
    q_ref[0, h, :, LANES:] = (_rope4(yr, cs, g[:, LANES:]) * inv).astype(BF16)


def _q_up(cqn, w, g, cs, tm):
  b, t, r = cqn.shape
  return pl.pallas_call(
      _q_up_kernel,
      grid=(b, t // tm),
      in_specs=[pl.BlockSpec((1, tm, r), lambda i, j: (i, j, 0)),
                _const_spec(w.shape), _const_spec(g.shape),
                pl.BlockSpec((tm, LANES), lambda i, j: (j, 0))],
      out_specs=pl.BlockSpec((1, N_HEADS, tm, HEAD_PAD), lambda i, j: (i, 0, j, 0)),
      out_shape=jax.ShapeDtypeStruct((b, N_HEADS, t, HEAD_PAD), BF16),
      compiler_params=_params("parallel", "parallel"),
      name="q_up",
  )(cqn, w, g, cs)


def _kv_up_kernel(c_ref, kr4_ref, w_ref, g_ref, cs_ref, k_ref, v_ref):
  c = c_ref[0].astype(BF16)
  kr4 = kr4_ref[0]
  g = g_ref[...]
  kr = _rope4(kr4, cs_ref[...], g[:, LANES:])
  ssr = 0.5 * jnp.sum(kr4 * kr4, axis=-1, keepdims=True)
  for h in range(N_HEADS):
    y = _mm(c, w_ref[:, h * HEAD_PAD:(h + 1) * HEAD_PAD])
    kn = y[:, :QK_NOPE]
    ss = jnp.sum(kn * kn, axis=-1, keepdims=True) + ssr
    inv = lax.rsqrt(ss * (1.0 / QK_HEAD) + EPS)
    k_ref[0, h, :, :LANES] = (kn * inv * g[:, :LANES]).astype(BF16)
    k_ref[0, h, :, LANES:] = (kr * inv).astype(BF16)
    v_ref[0, h] = y[:, QK_NOPE:].astype(BF16)


def _kv_up(ckv, kr4, w, g, cs, tm):
  b, t, r = ckv.shape
  return pl.pallas_call(
      _kv_up_kernel,
      grid=(b, t // tm),
      in_specs=[pl.BlockSpec((1, tm, r), lambda i, j: (i, j, 0)),
                pl.BlockSpec((1, tm, LANES), lambda i, j: (i, j, 0)),
                _const_spec(w.shape), _const_spec(g.shape),
                pl.BlockSpec((tm, LANES), lambda i, j: (j, 0))],
      out_specs=(pl.BlockSpec((1, N_HEADS, tm, HEAD_PAD), lambda i, j: (i, 0, j, 0)),
                 pl.BlockSpec((1, N_HEADS, tm, V_HEAD), lambda i, j: (i, 0, j, 0))),
      out_shape=(jax.ShapeDtypeStruct((b, N_HEADS, t, HEAD_PAD), BF16),
                 jax.ShapeDtypeStruct((b, N_HEADS, t, V_HEAD), BF16)),
      compiler_params=_params("parallel", "parallel"),
      name="kv_up",
  )(ckv, kr4, w, g, cs)


def _attn_kernel(q_ref, k_ref, v_ref, o_ref, m_sc, l_sc, acc_sc, *, tq, tk, n_keys, q_pos0, k_pos0):
  row0 = q_pos0 + pl.program_id(2) * tq
  q = q_ref[0, 0]
  keys_all = jnp.clip(((row0 >> CHUNK_SHIFT) + 1) * CHUNK - k_pos0, 0, n_keys)
  keys_any = jnp.clip((((row0 + tq - 1) >> CHUNK_SHIFT) + 1) * CHUNK - k_pos0, 0, n_keys)
  n_all = keys_all // tk
  n_any = (keys_any + tk - 1) // tk

  m_sc[...] = jnp.full(m_sc.shape, NEG_INF, F32)
  l_sc[...] = jnp.zeros(l_sc.shape, F32)
  acc_sc[...] = jnp.zeros(acc_sc.shape, F32)

  def step(j, masked):
    ks = pl.multiple_of(j * tk, tk)
    kt = k_ref[0, 0, pl.ds(ks, tk), :]
    vt = v_ref[0, 0, pl.ds(ks, tk), :]
    s = lax.dot_general(q, kt, (((1,), (1,)), ((), ())), preferred_element_type=F32)
    if masked:
      qc = (row0 + lax.broadcasted_iota(jnp.int32, (tq, 1), 0)) >> CHUNK_SHIFT
      kc = (k_pos0 + ks + lax.broadcasted_iota(jnp.int32, (1, tk), 1)) >> CHUNK_SHIFT
      s = jnp.where(kc <= qc, s, NEG_INF)
    m_prev = m_sc[...]
    m_new = jnp.maximum(m_prev, jnp.max(s, axis=-1, keepdims=True))
    alpha = jnp.exp2(m_prev - m_new)
    p = jnp.exp2(s - m_new)
    l_sc[...] = alpha * l_sc[...] + jnp.sum(p, axis=-1, keepdims=True)
    acc_sc[...] = alpha * acc_sc[...] + _mm(p.astype(BF16), vt)
    m_sc[...] = m_new

  def open_step(j, carry):
    step(j, False)
    return carry

  def masked_step(j, carry):
    step(j, True)
    return carry

  lax.fori_loop(0, n_all, open_step, 0)
  lax.fori_loop(n_all, n_any, masked_step, 0)
  o_ref[0] = (acc_sc[...] / l_sc[...]).astype(BF16)


def _attn(q, k, v, q_pos0, k_pos0, tq, tk):
  b, h, t, _ = q.shape
  n_keys = k.shape[2]
  assert q_pos0 >= k_pos0 and n_keys % tk == 0 and t % tq == 0
  return pl.pallas_call(
      functools.partial(_attn_kernel, tq=tq, tk=tk, n_keys=n_keys, q_pos0=q_pos0, k_pos0=k_pos0),
      grid=(b, h, t // tq),
      in_specs=[pl.BlockSpec((1, 1, tq, HEAD_PAD), lambda i, j, l: (i, j, l, 0)),
                pl.BlockSpec((1, 1, n_keys, HEAD_PAD), lambda i, j, l: (i, j, 0, 0)),
                pl.BlockSpec((1, 1, n_keys, V_HEAD), lambda i, j, l: (i, j, 0, 0))],
      out_specs=pl.BlockSpec((1, tq, V_HEAD), lambda i, j, l: (i, l, j)),
      out_shape=jax.ShapeDtypeStruct((b, t, h * V_HEAD), BF16),
      scratch_shapes=[pltpu.VMEM((tq, 1), F32), pltpu.VMEM((tq, 1), F32),
                      pltpu.VMEM((tq, V_HEAD), F32)],
      compiler_params=_params("parallel", "parallel", "arbitrary"),
      name="attn",
  )(q, k, v)


def _rglru_kernel(xr_ref, hist_ref, h0_ref, cw_ref, cb_ref, wg_ref, ba_ref, bx_ref, lam_ref,
                  hs_ref, hl_ref, xbuf, hcar, abuf, ubuf, *, tt):
  @pl.when(pl.program_id(1) == 0)
  def _():
    xbuf[0:SUBLANES, :] = hist_ref[0]
    hcar[...] = h0_ref[0]

  x = xr_ref[0].astype(F32)
  xbuf[SUBLANES:SUBLANES + tt, :] = x
  cw = cw_ref[...]
  xc = cb_ref[...] + xbuf[SUBLANES - 3:SUBLANES - 3 + tt, :] * cw[0:1, :]
  xc = xc + xbuf[SUBLANES - 2:SUBLANES - 2 + tt, :] * cw[1:2, :]
  xc = xc + xbuf[SUBLANES - 1:SUBLANES - 1 + tt, :] * cw[2:3, :]
  xc = xc + x * cw[3:4, :]
  xcb = xc.astype(BF16)

  z = -lam_ref[...]
  softplus = jnp.maximum(z, 0.0) + jnp.log1p(jnp.exp(-jnp.abs(z)))
  rate = -LRU_C * softplus

  blk = xc.shape[1] // RNN_BLOCKS
  row = lax.broadcasted_iota(jnp.int32, (tt, blk), 0) & (SUBLANES - 1)
  for n in range(RNN_BLOCKS):
    cols = slice(n * blk, (n + 1) * blk)
    gates = _mm(xcb[:, cols], wg_ref[n])
    r = _sigmoid(gates[:, :blk] + ba_ref[:, cols])
    i = _sigmoid(gates[:, blk:] + bx_ref[:, cols])
    log_a = rate[:, cols] * r
    th = jnp.tanh(log_a)
    a = jnp.exp(log_a)
    u = jnp.sqrt(-2.0 * th / (1.0 - th)) * (i * xc[:, cols])
    for s in (1, 2, 4):
      keep = row >= s
      a_prev = jnp.where(keep, pltpu.roll(a, s, axis=0), 1.0)
      u_prev = jnp.where(keep, pltpu.roll(u, s, axis=0), 0.0)
      u = a * u_prev + u
      a = a * a_prev
    abuf[:, cols] = a
    ubuf[:, cols] = u

  def group(g, h):
    r0 = pl.multiple_of(g * SUBLANES, SUBLANES)
    hr = abuf[pl.ds(r0, SUBLANES), :] * h + ubuf[pl.ds(r0, SUBLANES), :]
    ubuf[pl.ds(r0, SUBLANES), :] = hr
    return hr[SUBLANES - 1:SUBLANES, :]

  h = lax.fori_loop(0, tt // SUBLANES, group, hcar[...])
  hcar[...] = h
  hl_ref[0] = h
  hs_ref[0] = ubuf[...].astype(BF16)
  xbuf[0:SUBLANES, :] = xbuf[tt:tt + SUBLANES, :]


def _rglru(xr, hist, h0, cw, cb, wg, ba, bx, lam, tt):
  b, t, c = xr.shape
  return pl.pallas_call(
      functools.partial(_rglru_kernel, tt=tt),
      grid=(b, t // tt),
      in_specs=[pl.BlockSpec((1, tt, c), lambda i, j: (i, j, 0)),
                pl.BlockSpec((1, SUBLANES, c), lambda i, j: (i, 0, 0)),
                pl.BlockSpec((1, 1, c), lambda i, j: (i, 0, 0)),
                _const_spec(cw.shape), _const_spec(cb.shape), _const_spec(wg.shape),
                _const_spec(ba.shape), _const_spec(bx.shape), _const_spec(lam.shape)],
      out_specs=(pl.BlockSpec((1, tt, c), lambda i, j: (i, j, 0)),
                 pl.BlockSpec((1, 1, c), lambda i, j: (i, 0, 0))),
      out_shape=(jax.ShapeDtypeStruct((b, t, c), BF16),
                 jax.ShapeDtypeStruct((b, 1, c), F32)),
      scratch_shapes=[pltpu.VMEM((tt + SUBLANES, c), F32), pltpu.VMEM((1, c), F32),
                      pltpu.VMEM((tt, c), F32), pltpu.VMEM((tt, c), F32)],
      compiler_params=_params("parallel", "arbitrary"),
      name="rglru",
  )(xr, hist, h0, cw, cb, wg, ba, bx, lam)


def _out_kernel(x_ref, att_ref, hs_ref, ga_ref, gb_ref, wpa_ref, wpr_ref, wo_ref, g2_ref,
                wg_ref, wu_ref, wd_ref, y_ref, *, ff_chunk):
  o_att = _mm(att_ref[0], wpa_ref[...])
  o_rnn = _mm(hs_ref[0], wpr_ref[...])
  merged = ga_ref[0].astype(F32) * o_att + gb_ref[0].astype(F32) * o_rnn
  x1 = x_ref[0] + _mm(merged.astype(BF16), wo_ref[...])
  xn = _rms(x1, g2_ref[...]).astype(BF16)
  acc = x1
  for c0 in range(0, wg_ref.shape[1], ff_chunk):
    g = _mm(xn, wg_ref[:, c0:c0 + ff_chunk])
    u = _mm(xn, wu_ref[:, c0:c0 + ff_chunk])
    acc = acc + _mm((g * _sigmoid(g) * u).astype(BF16), wd_ref[c0:c0 + ff_chunk, :])
  y_ref[0] = acc


def _out(x, att, hs, ga, gb, wpa, wpr, wo, g2, wg, wu, wd, tm):
  b, t, d = x.shape
  d_ff = wg.shape[1]
  ff_chunk = 2 * LANES if d_ff % (2 * LANES) == 0 else d_ff
  row = lambda n: pl.BlockSpec((1, tm, n), lambda i, j: (i, j, 0))
  consts = [wpa, wpr, wo, g2, wg, wu, wd]
  return pl.pallas_call(
      functools.partial(_out_kernel, ff_chunk=ff_chunk),
      grid=(b, t // tm),
      in_specs=[row(d), row(att.shape[2]), row(hs.shape[2]), row(d), row(d)]
      + [_const_spec(a.shape) for a in consts],
      out_specs=row(d),
      out_shape=jax.ShapeDtypeStruct((b, t, d), F32),
      compiler_params=_params("parallel", "parallel"),
      name="out",
  )(x, att, hs, ga, gb, *consts)


def _quad(a):
  x1, x2 = a[..., :HALF], a[..., HALF:]
  return jnp.concatenate([x1, x2, -x2, x1], axis=-1)


def _gain_row(g):
  g1, g2 = g[QK_NOPE:QK_NOPE + HALF], g[QK_NOPE + HALF:]
  return jnp.concatenate([g[:QK_NOPE], g1, g2, g2, g1])[None, :]


def _rope_table(pos0, n):
  inv_freq = jnp.exp(-math.log(ROPE_THETA) * jnp.arange(HALF, dtype=F32) / HALF)
  ang = (pos0 + jnp.arange(n, dtype=jnp.int32)).astype(F32)[:, None] * inv_freq[None, :]
  c, s = jnp.cos(ang), jnp.sin(ang)
  return jnp.concatenate([c, c, s, s], axis=-1)


def _prep_layer(l, norm1_g, w_in, q_norm_g, w_uq, kv_norm_g, w_ukv, qk_q_g, qk_k_g,
                conv_w, conv_b, w_rg_a, b_rg_a, w_rg_x, b_rg_x, lru_lambda,
                w_proj_attn, w_proj_rnn, w_out, norm2_g, w_ffn_gate, w_ffn_up, w_ffn_down):
  q_lora, kv_lora = q_norm_g.shape[1], kv_norm_g.shape[1]
  d_rnn, d_model = conv_b.shape[1], norm1_g.shape[1]
  o_kr = q_lora + kv_lora
  o_x = o_kr + QK_ROPE
  w = w_in[l]
  w_in4 = jnp.concatenate([w[:, :o_kr], _quad(w[:, o_kr:o_x]), w[:, o_x:]], axis=1).astype(BF16)
  wq = w_uq[l].reshape(q_lora, N_HEADS, QK_HEAD)
  wq4 = jnp.concatenate([wq[..., :QK_NOPE], _quad(wq[..., QK_NOPE:])], axis=-1)
  wq4 = wq4.reshape(q_lora, N_HEADS * HEAD_PAD).astype(BF16)
  w_gate = jnp.concatenate([w_rg_a[l], w_rg_x[l]], axis=-1).astype(BF16)
  row = lambda a: a[l][None, :]
  return dict(
      widths=(q_lora, kv_lora, d_rnn, d_model),
      g1=row(norm1_g), w_in=w_in4, gq=row(q_norm_g), gkv=row(kv_norm_g),
      wq=wq4, gqh=_gain_row(qk_q_g[l]), wkv=w_ukv[l].astype(BF16), gkh=_gain_row(qk_k_g[l]),
      cw=conv_w[l], cb=row(conv_b), w_gate=w_gate, ba=row(b_rg_a), bx=row(b_rg_x),
      lam=row(lru_lambda),
      wpa=w_proj_attn[l].astype(BF16), wpr=w_proj_rnn[l].astype(BF16), wo=w_out[l].astype(BF16),
      g2=row(norm2_g), wg=w_ffn_gate[l].astype(BF16), wu=w_ffn_up[l].astype(BF16),
      wd=w_ffn_down[l].astype(BF16))


def _layer(x, pos0, past, p):
  b, t, _ = x.shape
  assert t >= SUBLANES and t % SUBLANES == 0
  tm = _tile(t, 512)
  cqn, ckv, kr4, kr, xr, xtail, ga, gb = _in_proj(
      x, p["g1"], p["w_in"], p["gq"], p["gkv"], p["widths"], tm)
  q = _q_up(cqn, p["wq"], p["gqh"], _rope_table(pos0, t), tm)

  if past is None:
    ckv_all, kr4_all, k_pos0 = ckv, kr4, pos0
    hist = jnp.zeros((b, SUBLANES, xr.shape[2]), F32)
    h0 = jnp.zeros((b, 1, xr.shape[2]), F32)
  else:
    past_ckv, past_krope, conv_hist, state_h = past
    k_pos0 = pos0 - past_ckv.shape[1]
    ckv_all = jnp.concatenate([past_ckv, ckv], axis=1)
    kr4_all = jnp.concatenate([_quad(past_krope), kr4], axis=1)
    hist = jnp.pad(conv_hist, ((0, 0), (SUBLANES - (CONV_W - 1), 0), (0, 0)))
    h0 = state_h[:, None, :]
  n_keys = ckv_all.shape[1]
  k, v = _kv_up(ckv_all, kr4_all, p["wkv"], p["gkh"], _rope_table(k_pos0, n_keys),
                _tile(n_keys, 512))
  att = _attn(q, k, v, pos0, k_pos0, _tile(t, 256), _tile(n_keys, 256) if n_keys > 2048 else n_keys)

  hs, hl = _rglru(xr, hist, h0, p["cw"], p["cb"], p["w_gate"], p["ba"], p["bx"], p["lam"],
                  _tile(t, 256))
  y = _out(x, att, hs, ga, gb, p["wpa"], p["wpr"], p["wo"], p["g2"], p["wg"], p["wu"], p["wd"], tm)
  return y, ckv, kr, xtail[:, SUBLANES - (CONV_W - 1):, :], hl[:, 0, :]


def kernel(x_prompt, x_sample, cache_ckv, cache_krope, state_conv, state_h,
           norm1_g, w_in, q_norm_g, w_uq, kv_norm_g, w_ukv, qk_q_g, qk_k_g,
           conv_w, conv_b, w_rg_a, b_rg_a, w_rg_x, b_rg_x, lru_lambda,
           w_proj_attn, w_proj_rnn, w_out, norm2_g, w_ffn_gate, w_ffn_up, w_ffn_down):
  depth = w_in.shape[0]
  past_len = cache_ckv.shape[2]
  y_p, y_s = x_prompt, x_sample
  outs_p, outs_s = [], []
  for l in range(depth):
    p = _prep_layer(l, norm1_g, w_in, q_norm_g, w_uq, kv_norm_g, w_ukv, qk_q_g, qk_k_g,
                    conv_w, conv_b, w_rg_a, b_rg_a, w_rg_x, b_rg_x, lru_lambda,
                    w_proj_attn, w_proj_rnn, w_out, norm2_g, w_ffn_gate, w_ffn_up, w_ffn_down)
    y_p, *new_p = _layer(y_p, 0, None, p)
    past = (cache_ckv[l], cache_krope[l], state_conv[l], state_h[l])
    y_s, *new_s = _layer(y_s, past_len, past, p)
    outs_p.append(new_p)
    outs_s.append(new_s)
  stack = lambda outs, i: jnp.stack([o[i] for o in outs])
  return (y_p, y_s,
          stack(outs_p, 0), stack(outs_p, 1), stack(outs_p, 2), stack(outs_p, 3),
          stack(outs_s, 0), stack(outs_s, 1), stack(outs_s, 2), stack(outs_s, 3))
```

```python
import functools
import math

import jax
import jax.numpy as jnp
from jax import lax
from jax.experimental import pallas as pl
from jax.experimental.pallas import tpu as pltpu

CHUNK = 64
N_HEADS = 8
QK_NOPE = 128
QK_ROPE = 64
QK_HEAD = QK_NOPE + QK_ROPE
V_HEAD = 128
ROPE_THETA = 10000.0
CONV_W = 4
LRU_C = 8.0
RNN_BLOCKS = 8
EPS = 1e-6
NEG_INF = -1e30
SCALE = QK_HEAD ** -0.5
LOG2E = 1.4426950408889634

LANES = 128
SUBLANES = 8
BF16_ROWS = 16
VMEM_LIMIT_BYTES = 58 * 1024 * 1024

HEAD_PAD = 2 * LANES
HALF = QK_ROPE // 2
CHUNK_SHIFT = CHUNK.bit_length() - 1
assert 1 << CHUNK_SHIFT == CHUNK

ATTN_HEADS_PER_STEP = 4
ATTN_TQ = 256
ATTN_TK = 512

F32 = jnp.float32
BF16 = jnp.bfloat16


def _tile(n, target, unit=BF16_ROWS):
  if n <= target:
    return n
  for t in range(target - target % unit, 0, -unit):
    if n % t == 0:
      return t
  return n


def _pad_axis(a, axis, n):
  if a.shape[axis] == n:
    return a
  widths = [(0, 0)] * a.ndim
  widths[axis] = (0, n - a.shape[axis])
  return jnp.pad(a, widths)


def _const_spec(shape):
  nd = len(shape)
  return pl.BlockSpec(shape, lambda *_: (0,) * nd, pipeline_mode=pl.Buffered(1))


def _params(*sem):
  return pltpu.CompilerParams(dimension_semantics=sem, vmem_limit_bytes=VMEM_LIMIT_BYTES)


def _rms(x, g):
  return x * lax.rsqrt(jnp.mean(x * x, axis=-1, keepdims=True) + EPS) * g


def _sigmoid(x):
  return 1.0 / (1.0 + jnp.exp(-x))


def _mm(a, b):
  return jnp.dot(a, b, preferred_element_type=F32)


def _in_proj_kernel(x_ref, g1_ref, w_ref, gq_ref, gkv_ref,
                    cqn_ref, ckv_ref, kr4_ref, kr_ref, xr_ref, xtail_ref, ga_ref, gb_ref,
                    *, widths):
  q_lora, kv_lora, d_rnn, d_model = widths
  xn = _rms(x_ref[0], g1_ref[...]).astype(BF16)
  off = 0

  def seg(n):
    nonlocal off
    r = _mm(xn, w_ref[:, off:off + n])
    off += n
    return r

  cqn_ref[0] = _rms(seg(q_lora), gq_ref[...]).astype(BF16)
  ckv_ref[0] = _rms(seg(kv_lora), gkv_ref[...])
  kr4 = seg(LANES)
  kr4_ref[0] = kr4
  kr_ref[0] = kr4[:, :QK_ROPE]
  xr = seg(d_rnn)
  xr_ref[0] = xr.astype(BF16)
  xtail_ref[0] = xr[xr.shape[0] - SUBLANES:, :]
  ga_ref[0] = _sigmoid(seg(d_model)).astype(BF16)
  gb_ref[0] = _sigmoid(seg(d_model)).astype(BF16)


def _in_proj(x, g1, w, gq, gkv, widths, tm):
  b, t, d = x.shape
  q_lora, kv_lora, d_rnn, d_model = widths
  row = lambda n: pl.BlockSpec((1, tm, n), lambda i, j: (i, j, 0))
  out_shape = (
      jax.ShapeDtypeStruct((b, t, q_lora), BF16),
      jax.ShapeDtypeStruct((b, t, kv_lora), F32),
      jax.ShapeDtypeStruct((b, t, LANES), F32),
      jax.ShapeDtypeStruct((b, t, QK_ROPE), F32),
      jax.ShapeDtypeStruct((b, t, d_rnn), BF16),
      jax.ShapeDtypeStruct((b, SUBLANES, d_rnn), F32),
      jax.ShapeDtypeStruct((b, t, d_model), BF16),
      jax.ShapeDtypeStruct((b, t, d_model), BF16),
  )
  out_specs = (row(q_lora), row(kv_lora), row(LANES), row(QK_ROPE), row(d_rnn),
               pl.BlockSpec((1, SUBLANES, d_rnn), lambda i, j: (i, 0, 0)),
               row(d_model), row(d_model))
  return pl.pallas_call(
      functools.partial(_in_proj_kernel, widths=widths),
      grid=(b, t // tm),
      in_specs=[row(d), _const_spec(g1.shape), _const_spec(w.shape),
                _const_spec(gq.shape), _const_spec(gkv.shape)],
      out_specs=out_specs,
      out_shape=out_shape,
      compiler_params=_params("parallel", "arbitrary"),
      name="in_proj",
  )(x, g1, w, gq, gkv)


_NT = (((1,), (1,)), ((), ()))


def _q_up_kernel(c_ref, w_ref, g_ref, cs_ref, q_ref, *, heads_per_dot):
  c = c_ref[0]
  tm = c.shape[0]
  g = jnp.concatenate([g_ref[...]] * (tm // LANES), axis=1)
  gn = g[:LANES]
  rot = cs_ref[...] * g[LANES:]
  rows = heads_per_dot * HEAD_PAD
  for h0 in range(0, N_HEADS, heads_per_dot):
    y_all = lax.dot_general(w_ref[h0 * HEAD_PAD:h0 * HEAD_PAD + rows, :], c, _NT,
                            preferred_element_type=F32)
    for d in range(heads_per_dot):
      h = h0 + d
      yn = y_all[d * HEAD_PAD:d * HEAD_PAD + LANES]
      yr = y_all[d * HEAD_PAD + LANES:(d + 1) * HEAD_PAD]
      ss = jnp.sum(yn * yn, axis=0, keepdims=True) + 0.5 * jnp.sum(yr * yr, axis=0, keepdims=True)
      inv = lax.rsqrt(ss * (1.0 / QK_HEAD) + EPS) * (SCALE * LOG2E)
      t = yr * rot
      q_ref[0, h, :LANES, :] = (yn * gn * inv).astype(BF16)
      q_ref[0, h, LANES:LANES + QK_ROPE, :] = ((t[:QK_ROPE] + t[QK_ROPE:]) * inv).astype(BF16)
      q_ref[0, h, LANES + QK_ROPE:, :] = jnp.zeros((HEAD_PAD - LANES - QK_ROPE, tm), BF16)


def _q_up(cqn, w_t, g_t, cs_t, tm):
  b, t, r = cqn.shape
  return pl.pallas_call(
      functools.partial(_q_up_kernel, heads_per_dot=2),
      grid=(b, t // tm),
      in_specs=[pl.BlockSpec((1, tm, r), lambda i, j: (i, j, 0)),
                _const_spec(w_t.shape), _const_spec(g_t.shape),
                pl.BlockSpec((LANES, tm), lambda i, j: (0, j))],
      out_specs=pl.BlockSpec((1, N_HEADS, HEAD_PAD, tm), lambda i, j: (i, 0, 0, j)),
      out_shape=jax.ShapeDtypeStruct((b, N_HEADS, HEAD_PAD, t), BF16),
      compiler_params=_params("parallel", "parallel"),
      name="q_up",
  )(cqn, w_t, g_t, cs_t)


def _rope4(t4, cs, g4):
  t = t4 * cs * g4
  r = t + pltpu.roll(t, QK_ROPE, axis=1)
  lane = lax.broadcasted_iota(jnp.int32, r.shape, 1)
  return jnp.where(lane < QK_ROPE, r, 0.0)


def _kv_up_kernel(c_ref, kr4_ref, wk_ref, wv_ref, g_ref, cs_ref, k_ref, vt_ref):
  c = c_ref[0].astype(BF16)
  kr4 = kr4_ref[0]
  g = g_ref[...]
  kr = _rope4(kr4, cs_ref[...], g[:, LANES:])
  ssr = 0.5 * jnp.sum(kr4 * kr4, axis=-1, keepdims=True)
  for h0 in range(0, N_HEADS, 2):
    y = _mm(c, wk_ref[:, h0 * QK_NOPE:(h0 + 2) * QK_NOPE])
    for d in range(2):
      kn = y[:, d * QK_NOPE:(d + 1) * QK_NOPE]
      ss = jnp.sum(kn * kn, axis=-1, keepdims=True) + ssr
      inv = lax.rsqrt(ss * (1.0 / QK_HEAD) + EPS)
      k_ref[0, h0 + d, :, :LANES] = (kn * inv * g[:, :LANES]).astype(BF16)
      k_ref[0, h0 + d, :, LANES:] = (kr * inv).astype(BF16)
  vt = lax.dot_general(wv_ref[...], c, _NT, preferred_element_type=F32)
  for h in range(N_HEADS):
    vt_ref[0, h] = vt[h * V_HEAD:(h + 1) * V_HEAD].astype(BF16)


def _kv_up(ckv, kr4, wk, wv_t, g, cs, tm):
  b, t, r = ckv.shape
  return pl.pallas_call(
      _kv_up_kernel,
      grid=(b, t // tm),
      in_specs=[pl.BlockSpec((1, tm, r), lambda i, j: (i, j, 0)),
                pl.BlockSpec((1, tm, LANES), lambda i, j: (i, j, 0)),
                _const_spec(wk.shape), _const_spec(wv_t.shape), _const_spec(g.shape),
                pl.BlockSpec((tm, LANES), lambda i, j: (j, 0))],
      out_specs=(pl.BlockSpec((1, N_HEADS, tm, HEAD_PAD), lambda i, j: (i, 0, j, 0)),
                 pl.BlockSpec((1, N_HEADS, V_HEAD, tm), lambda i, j: (i, 0, 0, j))),
      out_shape=(jax.ShapeDtypeStruct((b, N_HEADS, t, HEAD_PAD), BF16),
                 jax.ShapeDtypeStruct((b, N_HEADS, V_HEAD, t), BF16)),
      compiler_params=_params("parallel", "parallel"),
      name="kv_up",
  )(ckv, kr4, wk, wv_t, g, cs)


def _attn_kernel(q_ref, k_ref, vt_ref, o_ref, acc_sc, *, hp, tq, tk, n_valid, q_pos0, k_pos0):
  row0 = q_pos0 + pl.program_id(2) * tq
  keys_all = jnp.clip(((row0 >> CHUNK_SHIFT) + 1) * CHUNK - k_pos0, 0, n_valid)
  keys_any = jnp.clip((((row0 + tq - 1) >> CHUNK_SHIFT) + 1) * CHUNK - k_pos0, 0, n_valid)
  n_all = keys_all // tk
  n_any = (keys_any + tk - 1) // tk
  q_chunk = (row0 + lax.broadcasted_iota(jnp.int32, (1, tq), 1)) >> CHUNK_SHIFT
  acc_sc[...] = jnp.zeros(acc_sc.shape, F32)

  def tile(j, carry, masked):
    ks = pl.multiple_of(j * tk, tk)
    if masked:
      k_idx = ks + lax.broadcasted_iota(jnp.int32, (tk, 1), 0)
      visible = (((k_pos0 + k_idx) >> CHUNK_SHIFT) <= q_chunk) & (k_idx < n_valid)
    sts = [_mm(k_ref[0, h, pl.ds(ks, tk), :], q_ref[0, h]) for h in range(hp)]
    out, ps, alphas = [], [], []
    for h in range(hp):
      m, l = carry[2 * h], carry[2 * h + 1]
      st = jnp.where(visible, sts[h], NEG_INF) if masked else sts[h]
      m_new = jnp.maximum(m, jnp.max(st, axis=0, keepdims=True))
      alpha = jnp.exp2(m - m_new)
      p = jnp.exp2(st - m_new)
      out += [m_new, alpha * l + jnp.sum(p, axis=0, keepdims=True)]
      ps.append(p.astype(BF16))
      alphas.append(alpha)
    for h in range(hp):
      acc_sc[h] = alphas[h] * acc_sc[h] + _mm(vt_ref[0, h, :, pl.ds(ks, tk)], ps[h])
    return tuple(out)

  init = (jnp.full((1, tq), NEG_INF, F32), jnp.zeros((1, tq), F32)) * hp
  carry = lax.fori_loop(0, n_all, functools.partial(tile, masked=False), init)
  carry = lax.fori_loop(n_all, n_any, functools.partial(tile, masked=True), carry)
  for h in range(hp):
    o = acc_sc[h] * (1.0 / carry[2 * h + 1])
    o_ref[0, :, h * V_HEAD:(h + 1) * V_HEAD] = o.T.astype(BF16)


def _attn(q_t, k, v_t, n_valid, q_pos0, k_pos0, hp, tq, tk):
  b, h, _, t = q_t.shape
  n_keys = k.shape[2]
  assert q_pos0 >= k_pos0 and n_keys % tk == 0 and t % tq == 0 and h % hp == 0
  return pl.pallas_call(
      functools.partial(_attn_kernel, hp=hp, tq=tq, tk=tk, n_valid=n_valid,
                        q_pos0=q_pos0, k_pos0=k_pos0),
      grid=(b, h // hp, t // tq),
      in_specs=[pl.BlockSpec((1, hp, HEAD_PAD, tq), lambda i, j, l: (i, j, 0, l)),
                pl.BlockSpec((1, hp, n_keys, HEAD_PAD), lambda i, j, l: (i, j, 0, 0)),
                pl.BlockSpec((1, hp, V_HEAD, n_keys), lambda i, j, l: (i, j, 0, 0))],
      out_specs=pl.BlockSpec((1, tq, hp * V_HEAD), lambda i, j, l: (i, l, j)),
      out_shape=jax.ShapeDtypeStruct((b, t, h * V_HEAD), BF16),
      scratch_shapes=[pltpu.VMEM((hp, V_HEAD, tq), F32)],
      compiler_params=_params("parallel", "parallel", "arbitrary"),
      name="attn",
  )(q_t, k, v_t)


def _rglru_kernel(xr_ref, hist_ref, h0_ref, cw_ref, cb_ref, wg_ref, ba_ref, bx_ref, lam_ref,
                  hs_ref, hl_ref, xbuf, hcar, abuf, ubuf, *, tt):
  @pl.when(pl.program_id(1) == 0)
  def _():
    xbuf[0:SUBLANES, :] = hist_ref[0]
    hcar[...] = h0_ref[0]

  x = xr_ref[0].astype(F32)
  xbuf[SUBLANES:SUBLANES + tt, :] = x
  cw = cw_ref[...]
  xc = cb_ref[...] + xbuf[SUBLANES - 3:SUBLANES - 3 + tt, :] * cw[0:1, :]
  xc = xc + xbuf[SUBLANES - 2:SUBLANES - 2 + tt, :] * cw[1:2, :]
  xc = xc + xbuf[SUBLANES - 1:SUBLANES - 1 + tt, :] * cw[2:3, :]
  xc = xc + x * cw[3:4, :]
  xcb = xc.astype(BF16)

  z = -lam_ref[...]
  softplus = jnp.maximum(z, 0.0) + jnp.log1p(jnp.exp(-jnp.abs(z)))
  rate = -LRU_C * softplus

  blk = xc.shape[1] // RNN_BLOCKS
  row = lax.broadcasted_iota(jnp.int32, (tt, blk), 0) & (SUBLANES - 1)
  for n in range(RNN_BLOCKS):
    cols = slice(n * blk, (n + 1) * blk)
    gates = _mm(xcb[:, cols], wg_ref[n])
    r = _sigmoid(gates[:, :blk] + ba_ref[:, cols])
    i = _sigmoid(gates[:, blk:] + bx_ref[:, cols])
    log_a = rate[:, cols] * r
    th = jnp.tanh(log_a)
    a = jnp.exp(log_a)
    u = jnp.sqrt(-2.0 * th / (1.0 - th)) * (i * xc[:, cols])
    for s in (1, 2, 4):
      keep = row >= s
      a_prev = jnp.where(keep, pltpu.roll(a, s, axis=0), 1.0)
      u_prev = jnp.where(keep, pltpu.roll(u, s, axis=0), 0.0)
      u = a * u_prev + u
      a = a * a_prev
    abuf[:, cols] = a
    ubuf[:, cols] = u

  def group(g, h):
    r0 = pl.multiple_of(g * SUBLANES, SUBLANES)
    hr = abuf[pl.ds(r0, SUBLANES), :] * h + ubuf[pl.ds(r0, SUBLANES), :]
    ubuf[pl.ds(r0, SUBLANES), :] = hr
    return hr[SUBLANES - 1:SUBLANES, :]

  h = lax.fori_loop(0, tt // SUBLANES, group, hcar[...])
  hcar[...] = h
  hl_ref[0] = h
  hs_ref[0] = ubuf[...].astype(BF16)
  xbuf[0:SUBLANES, :] = xbuf[tt:tt + SUBLANES, :]


def _rglru(xr, hist, h0, cw, cb, wg, ba, bx, lam, tt):
  b, t, c = xr.shape
  return pl.pallas_call(
      functools.partial(_rglru_kernel, tt=tt),
      grid=(b, t // tt),
      in_specs=[pl.BlockSpec((1, tt, c), lambda i, j: (i, j, 0)),
                pl.BlockSpec((1, SUBLANES, c), lambda i, j: (i, 0, 0)),
                pl.BlockSpec((1, 1, c), lambda i, j: (i, 0, 0)),
                _const_spec(cw.shape), _const_spec(cb.shape), _const_spec(wg.shape),
                _const_spec(ba.shape), _const_spec(bx.shape), _const_spec(lam.shape)],
      out_specs=(pl.BlockSpec((1, tt, c), lambda i, j: (i, j, 0)),
                 pl.BlockSpec((1, 1, c), lambda i, j: (i, 0, 0))),
      out_shape=(jax.ShapeDtypeStruct((b, t, c), BF16),
                 jax.ShapeDtypeStruct((b, 1, c), F32)),
      scratch_shapes=[pltpu.VMEM((tt + SUBLANES, c), F32), pltpu.VMEM((1, c), F32),
                      pltpu.VMEM((tt, c), F32), pltpu.VMEM((tt, c), F32)],
      compiler_params=_params("parallel", "arbitrary"),
      name="rglru",
  )(xr, hist, h0, cw, cb, wg, ba, bx, lam)


def _out_kernel(x_ref, att_ref, hs_ref, ga_ref, gb_ref, wpa_ref, wpr_ref, wo_ref, g2_ref,
                wg_ref, wu_ref, wd_ref, y_ref, *, ff_chunk):
  o_att = _mm(att_ref[0], wpa_ref[...])
  o_rnn = _mm(hs_ref[0], wpr_ref[...])
  merged = ga_ref[0].astype(F32) * o_att + gb_ref[0].astype(F32) * o_rnn
  x1 = x_ref[0] + _mm(merged.astype(BF16), wo_ref[...])
  xn = _rms(x1, g2_ref[...]).astype(BF16)
  acc = x1
  for c0 in range(0, wg_ref.shape[1], ff_chunk):
    g = _mm(xn, wg_ref[:, c0:c0 + ff_chunk])
    u = _mm(xn, wu_ref[:, c0:c0 + ff_chunk])
    acc = acc + _mm((g * _sigmoid(g) * u).astype(BF16), wd_ref[c0:c0 + ff_chunk, :])
  y_ref[0] = acc


def _out(x, att, hs, ga, gb, wpa, wpr, wo, g2, wg, wu, wd, tm):
  b, t, d = x.shape
  d_ff = wg.shape[1]
  ff_chunk = 2 * LANES if d_ff % (2 * LANES) == 0 else d_ff
  row = lambda n: pl.BlockSpec((1, tm, n), lambda i, j: (i, j, 0))
  consts = [wpa, wpr, wo, g2, wg, wu, wd]
  return pl.pallas_call(
      functools.partial(_out_kernel, ff_chunk=ff_chunk),
      grid=(b, t // tm),
      in_specs=[row(d), row(att.shape[2]), row(hs.shape[2]), row(d), row(d)]
      + [_const_spec(a.shape) for a in consts],
      out_specs=row(d),
      out_shape=jax.ShapeDtypeStruct((b, t, d), F32),
      compiler_params=_params("parallel", "parallel"),
      name="out",
  )(x, att, hs, ga, gb, *consts)


def _quad(a):
  x1, x2 = a[..., :HALF], a[..., HALF:]
  return jnp.concatenate([x1, x2, -x2, x1], axis=-1)


def _gain_row(g):
  g1, g2 = g[QK_NOPE:QK_NOPE + HALF], g[QK_NOPE + HALF:]
  return jnp.concatenate([g[:QK_NOPE], g1, g2, g2, g1])[None, :]


def _rope_table(pos0, n):
  inv_freq = jnp.exp(-math.log(ROPE_THETA) * jnp.arange(HALF, dtype=F32) / HALF)
  ang = (pos0 + jnp.arange(n, dtype=jnp.int32)).astype(F32)[:, None] * inv_freq[None, :]
  c, s = jnp.cos(ang), jnp.sin(ang)
  return jnp.concatenate([c, c, s, s], axis=-1)


def _prep_layer(l, norm1_g, w_in, q_norm_g, w_uq, kv_norm_g, w_ukv, qk_q_g, qk_k_g,
                conv_w, conv_b, w_rg_a, b_rg_a, w_rg_x, b_rg_x, lru_lambda,
                w_proj_attn, w_proj_rnn, w_out, norm2_g, w_ffn_gate, w_ffn_up, w_ffn_down):
  q_lora, kv_lora = q_norm_g.shape[1], kv_norm_g.shape[1]
  d_rnn, d_model = conv_b.shape[1], norm1_g.shape[1]
  o_kr = q_lora + kv_lora
  o_x = o_kr + QK_ROPE
  w = w_in[l]
  w_in4 = jnp.concatenate([w[:, :o_kr], _quad(w[:, o_kr:o_x]), w[:, o_x:]], axis=1).astype(BF16)
  wq = w_uq[l].reshape(q_lora, N_HEADS, QK_HEAD)
  wq4 = jnp.concatenate([wq[..., :QK_NOPE], _quad(wq[..., QK_NOPE:])], axis=-1)
  wq_t = wq4.reshape(q_lora, N_HEADS * HEAD_PAD).T.astype(BF16)
  wkv = w_ukv[l].reshape(kv_lora, N_HEADS, QK_NOPE + V_HEAD)
  wk = wkv[..., :QK_NOPE].reshape(kv_lora, N_HEADS * QK_NOPE).astype(BF16)
  wv_t = wkv[..., QK_NOPE:].reshape(kv_lora, N_HEADS * V_HEAD).T.astype(BF16)
  w_gate = jnp.concatenate([w_rg_a[l], w_rg_x[l]], axis=-1).astype(BF16)
  row = lambda a: a[l][None, :]
  return dict(
      widths=(q_lora, kv_lora, d_rnn, d_model),
      g1=row(norm1_g), w_in=w_in4, gq=row(q_norm_g), gkv=row(kv_norm_g),
      wq_t=wq_t, gq_t=jnp.broadcast_to(_gain_row(qk_q_g[l]).T, (HEAD_PAD, LANES)),
      wk=wk, wv_t=wv_t, gkh=_gain_row(qk_k_g[l]),
      cw=conv_w[l], cb=row(conv_b), w_gate=w_gate, ba=row(b_rg_a), bx=row(b_rg_x),
      lam=row(lru_lambda),
      wpa=w_proj_attn[l].astype(BF16), wpr=w_proj_rnn[l].astype(BF16), wo=w_out[l].astype(BF16),
      g2=row(norm2_g), wg=w_ffn_gate[l].astype(BF16), wu=w_ffn_up[l].astype(BF16),
      wd=w_ffn_down[l].astype(BF16))


def _layer(x, pos0, past, p):
  b, t, _ = x.shape
  assert t >= SUBLANES and t % SUBLANES == 0
  tm = _tile(t, 512)
  cqn, ckv, kr4, kr, xr, xtail, ga, gb = _in_proj(
      x, p["g1"], p["w_in"], p["gq"], p["gkv"], p["widths"], tm)

  t_pad = -(-t // LANES) * LANES
  q_t = _q_up(_pad_axis(cqn, 1, t_pad), p["wq_t"], p["gq_t"], _rope_table(pos0, t_pad).T,
              _tile(t_pad, 512, LANES))

  if past is None:
    ckv_all, kr4_all, k_pos0 = ckv, kr4, pos0
    hist = jnp.zeros((b, SUBLANES, xr.shape[2]), F32)
    h0 = jnp.zeros((b, 1, xr.shape[2]), F32)
  else:
    past_ckv, past_krope, conv_hist, state_h = past
    k_pos0 = pos0 - past_ckv.shape[1]
    ckv_all = jnp.concatenate([past_ckv, ckv], axis=1)
    kr4_all = jnp.concatenate([_quad(past_krope), kr4], axis=1)
    hist = jnp.pad(conv_hist, ((0, 0), (SUBLANES - (CONV_W - 1), 0), (0, 0)))
    h0 = state_h[:, None, :]
  n_keys = ckv_all.shape[1]
  n_pad = -(-n_keys // LANES) * LANES
  k, v_t = _kv_up(_pad_axis(ckv_all, 1, n_pad), _pad_axis(kr4_all, 1, n_pad), p["wk"], p["wv_t"],
                  p["gkh"], _rope_table(k_pos0, n_pad), _tile(n_pad, 512, LANES))
  att = _attn(q_t, k, v_t, n_keys, pos0, k_pos0, hp=ATTN_HEADS_PER_STEP,
              tq=_tile(t_pad, ATTN_TQ, LANES), tk=_tile(n_pad, ATTN_TK, LANES))[:, :t]

  hs, hl = _rglru(xr, hist, h0, p["cw"], p["cb"], p["w_gate"], p["ba"], p["bx"], p["lam"],
                  _tile(t, 256))
  y = _out(x, att, hs, ga, gb, p["wpa"], p["wpr"], p["wo"], p["g2"], p["wg"], p["wu"], p["wd"], tm)
  return y, ckv, kr, xtail[:, SUBLANES - (CONV_W - 1):, :], hl[:, 0, :]


def kernel(x_prompt, x_sample, cache_ckv, cache_krope, state_conv, state_h,
           norm1_g, w_in, q_norm_g, w_uq, kv_norm_g, w_ukv, qk_q_g, qk_k_g,
           conv_w, conv_b, w_rg_a, b_rg_a, w_rg_x, b_rg_x, lru_lambda,
           w_proj_attn, w_proj_rnn, w_out, norm2_g, w_ffn_gate, w_ffn_up, w_ffn_down):
  depth = w_in.shape[0]
  past_len = cache_ckv.shape[2]
  y_p, y_s = x_prompt, x_sample
  outs_p, outs_s = [], []
  for l in range(depth):
    p = _prep_layer(l, norm1_g, w_in, q_norm_g, w_uq, kv_norm_g, w_ukv, qk_q_g, qk_k_g,
                    conv_w, conv_b, w_rg_a, b_rg_a, w_rg_x, b_rg_x, lru_lambda,
                    w_proj_attn, w_proj_rnn, w_out, norm2_g, w_ffn_gate, w_ffn_up, w_ffn_down)
    y_p, *new_p = _layer(y_p, 0, None, p)
    past = (cache_ckv[l], cache_krope[l], state_conv[l], state_h[l])
    y_s, *new_s = _layer(y_s, past_len, past, p)
    outs_p.append(new_p)
    outs_s.append(new_s)
  stack = lambda outs, i: jnp.stack([o[i] for o in outs])
  return (y_p, y_s,
          stack(outs_p, 0), stack(outs_p, 1), stack(outs_p, 2), stack(outs_p, 3),
          stack(outs_s, 0), stack(outs_s, 1), stack(outs_s, 2), stack(outs_s, 3))
```

```python
import functools
import math

import jax
import jax.numpy as jnp
from jax import lax
from jax.experimental import pallas as pl
from jax.experimental.pallas import tpu as pltpu

CHUNK = 64
N_HEADS = 8
QK_NOPE = 128
QK_ROPE = 64
QK_HEAD = QK_NOPE + QK_ROPE
V_HEAD = 128
ROPE_THETA = 10000.0
CONV_W = 4
LRU_C = 8.0
RNN_BLOCKS = 8
EPS = 1e-6
NEG_INF = -1e30
SCALE = QK_HEAD ** -0.5
LOG2E = 1.4426950408889634
F32_MIN_NORMAL = 1.1754943508222875e-38

LANES = 128
SUBLANES = 8
BF16_ROWS = 16
VMEM_LIMIT_BYTES = 58 * 1024 * 1024

HEAD_PAD = 2 * LANES
HALF = QK_ROPE // 2
CHUNK_SHIFT = CHUNK.bit_length() - 1
assert 1 << CHUNK_SHIFT == CHUNK

ATTN_HEADS_PER_STEP = 4
ATTN_TQ = 512
ATTN_TK = 512

F32 = jnp.float32
BF16 = jnp.bfloat16


def _tile(n, target, unit=BF16_ROWS):
  if n <= target:
    return n
  for t in range(target - target % unit, 0, -unit):
    if n % t == 0:
      return t
  return n


def _pad_axis(a, axis, n):
  if a.shape[axis] == n:
    return a
  widths = [(0, 0)] * a.ndim
  widths[axis] = (0, n - a.shape[axis])
  return jnp.pad(a, widths)


def _const_spec(shape):
  nd = len(shape)
  return pl.BlockSpec(shape, lambda *_: (0,) * nd, pipeline_mode=pl.Buffered(1))


def _params(*sem):
  return pltpu.CompilerParams(dimension_semantics=sem, vmem_limit_bytes=VMEM_LIMIT_BYTES)


def _rms(x, g):
  return x * lax.rsqrt(jnp.mean(x * x, axis=-1, keepdims=True) + EPS) * g


def _sigmoid(x):
  return 0.5 * jnp.tanh(0.5 * x) + 0.5


def _sqrt_nonneg(x):
  return x * lax.rsqrt(jnp.maximum(x, F32_MIN_NORMAL))


def _mm(a, b):
  return jnp.dot(a, b, preferred_element_type=F32)


HIST_ROWS = (CONV_W - 1) * SUBLANES


def _in_rnn_kernel(x_ref, g1_ref, w_ref, gq_ref, gkv_ref, hist_ref, h0_ref, cw_ref, cb_ref,
                   wgate_ref, ba_ref, bx_ref, lam_ref,
                   cqn_ref, ckv_ref, kr4_ref, kr_ref, xtail_ref, ga_ref, gb_ref, hs_ref, hl_ref,
                   xs, xcs, ras, ias, hcar, *, widths, tm):
  q_lora, kv_lora, d_rnn, d_model = widths
  o_kv, o_kr = q_lora, q_lora + kv_lora
  o_x = o_kr + LANES
  o_ga = o_x + d_rnn
  o_gb = o_ga + d_model
  rows = tm * SUBLANES
  blk = d_rnn // RNN_BLOCKS
  assert blk == LANES and RNN_BLOCKS == SUBLANES

  @pl.when(pl.program_id(1) == 0)
  def _():
    xs[0:HIST_ROWS, :] = hist_ref[0]
    hcar[...] = h0_ref[0]

  xn = _rms(x_ref[0], g1_ref[...]).astype(BF16)
  seg = lambda off, n: _mm(xn, w_ref[:, off:off + n])
  strided = lambda n: pl.ds(n, tm, stride=SUBLANES)
  tmajor = lambda v: v.reshape(tm, SUBLANES, LANES)

  xr = seg(o_x, d_rnn)
  xtail_ref[0] = xr[tm - SUBLANES:, :]
  for n in range(RNN_BLOCKS):
    xs[pl.ds(HIST_ROWS + n, tm, stride=SUBLANES), :] = xr[:, n * blk:(n + 1) * blk]
  xc = cb_ref[...][None] + tmajor(xs[0:rows, :]) * cw_ref[0][None]
  for j in range(1, CONV_W):
    xc = xc + tmajor(xs[j * SUBLANES:j * SUBLANES + rows, :]) * cw_ref[j][None]
  xcs[...] = xc.reshape(rows, LANES)

  cqn_ref[0] = _rms(seg(0, q_lora), gq_ref[...]).astype(BF16)
  ckv_ref[0] = _rms(seg(o_kv, kv_lora), gkv_ref[...])
  kr4 = seg(o_kr, LANES)
  kr4_ref[0] = kr4
  kr_ref[0] = kr4[:, :QK_ROPE]
  for n in range(RNN_BLOCKS):
    gates = _mm(xcs[strided(n), :].astype(BF16), wgate_ref[n])
    ras[strided(n), :] = gates[:, :blk]
    ias[strided(n), :] = gates[:, blk:]
  ga_ref[0] = _sigmoid(seg(o_ga, d_model)).astype(BF16)
  gb_ref[0] = _sigmoid(seg(o_gb, d_model)).astype(BF16)

  z = -lam_ref[...]
  softplus = jnp.maximum(z, 0.0) + jnp.log1p(jnp.exp(-jnp.abs(z)))
  half_rate = (-0.5 * LRU_C) * softplus
  r = _sigmoid(tmajor(ras[...]) + ba_ref[...][None])
  i = _sigmoid(tmajor(ias[...]) + bx_ref[...][None])
  th = jnp.tanh(half_rate[None] * r)
  rden = 1.0 / (1.0 - th)
  a = (1.0 + th) * rden
  ras[...] = a.reshape(rows, LANES)
  u = _sqrt_nonneg((th * (-2.0 * rden)) * (1.0 + a)) * (i * tmajor(xcs[...]))
  ias[...] = u.reshape(rows, LANES)

  h = hcar[...]
  for t in range(tm):
    step = slice(t * SUBLANES, (t + 1) * SUBLANES)
    h = ras[step, :] * h + ias[step, :]
    ias[step, :] = h
  hcar[...] = h
  hl_ref[0] = h
  for n in range(RNN_BLOCKS):
    hs_ref[0, :, n * blk:(n + 1) * blk] = ias[strided(n), :].astype(BF16)
  xs[0:HIST_ROWS, :] = xs[rows:rows + HIST_ROWS, :]


def _in_rnn(x, hist, h0, p, tm):
  b, t, d = x.shape
  widths = p["widths"]
  q_lora, kv_lora, d_rnn, d_model = widths
  row = lambda n: pl.BlockSpec((1, tm, n), lambda i, j: (i, j, 0))
  per_batch = lambda r: pl.BlockSpec((1, r, LANES), lambda i, j: (i, 0, 0))
  consts = [p["g1"], p["w_in"], p["gq"], p["gkv"]]
  rnn_consts = [p["cw"], p["cb"], p["w_gate"], p["ba"], p["bx"], p["lam"]]
  out_shape = (
      jax.ShapeDtypeStruct((b, t, q_lora), BF16),
      jax.ShapeDtypeStruct((b, t, kv_lora), F32),
      jax.ShapeDtypeStruct((b, t, LANES), F32),
      jax.ShapeDtypeStruct((b, t, QK_ROPE), F32),
      jax.ShapeDtypeStruct((b, SUBLANES, d_rnn), F32),
      jax.ShapeDtypeStruct((b, t, d_model), BF16),
      jax.ShapeDtypeStruct((b, t, d_model), BF16),
      jax.ShapeDtypeStruct((b, t, d_rnn), BF16),
      jax.ShapeDtypeStruct((b, SUBLANES, LANES), F32),
  )
  out_specs = (row(q_lora), row(kv_lora), row(LANES), row(QK_ROPE),
               pl.BlockSpec((1, SUBLANES, d_rnn), lambda i, j: (i, 0, 0)),
               row(d_model), row(d_model), row(d_rnn), per_batch(SUBLANES))
  rows = tm * SUBLANES
  return pl.pallas_call(
      functools.partial(_in_rnn_kernel, widths=widths, tm=tm),
      grid=(b, t // tm),
      in_specs=[row(d)] + [_const_spec(a.shape) for a in consts]
      + [per_batch(HIST_ROWS), per_batch(SUBLANES)] + [_const_spec(a.shape) for a in rnn_consts],
      out_specs=out_specs,
      out_shape=out_shape,
      scratch_shapes=[pltpu.VMEM((rows + HIST_ROWS, LANES), F32), pltpu.VMEM((rows, LANES), F32),
                      pltpu.VMEM((rows, LANES), F32), pltpu.VMEM((rows, LANES), F32),
                      pltpu.VMEM((SUBLANES, LANES), F32)],
      compiler_params=_params("parallel", "arbitrary"),
      name="in_rnn",
  )(x, *consts, hist, h0, *rnn_consts)


_NT = (((1,), (1,)), ((), ()))


def _q_up_kernel(c_ref, w_ref, g_ref, cs_ref, q_ref, *, heads_per_dot):
  c = c_ref[0]
  tm = c.shape[0]
  g = jnp.concatenate([g_ref[...]] * (tm // LANES), axis=1)
  gn = g[:LANES]
  rot = cs_ref[...] * g[LANES:]
  rows = heads_per_dot * HEAD_PAD
  for h0 in range(0, N_HEADS, heads_per_dot):
    y_all = lax.dot_general(w_ref[h0 * HEAD_PAD:h0 * HEAD_PAD + rows, :], c, _NT,
                            preferred_element_type=F32)
    for d in range(heads_per_dot):
      h = h0 + d
      yn = y_all[d * HEAD_PAD:d * HEAD_PAD + LANES]
      yr = y_all[d * HEAD_PAD + LANES:(d + 1) * HEAD_PAD]
      ss = jnp.sum(yn * yn, axis=0, keepdims=True) + 0.5 * jnp.sum(yr * yr, axis=0, keepdims=True)
      inv = lax.rsqrt(ss * (1.0 / QK_HEAD) + EPS) * (SCALE * LOG2E)
      t = yr * rot
      q_ref[0, h, :LANES, :] = (yn * gn * inv).astype(BF16)
      q_ref[0, h, LANES:LANES + QK_ROPE, :] = ((t[:QK_ROPE] + t[QK_ROPE:]) * inv).astype(BF16)
      q_ref[0, h, LANES + QK_ROPE:, :] = jnp.zeros((HEAD_PAD - LANES - QK_ROPE, tm), BF16)


def _q_up(cqn, w_t, g_t, cs_t, tm):
  b, t, r = cqn.shape
  return pl.pallas_call(
      functools.partial(_q_up_kernel, heads_per_dot=2),
      grid=(b, t // tm),
      in_specs=[pl.BlockSpec((1, tm, r), lambda i, j: (i, j, 0)),
                _const_spec(w_t.shape), _const_spec(g_t.shape),
                pl.BlockSpec((LANES, tm), lambda i, j: (0, j))],
      out_specs=pl.BlockSpec((1, N_HEADS, HEAD_PAD, tm), lambda i, j: (i, 0, 0, j)),
      out_shape=jax.ShapeDtypeStruct((b, N_HEADS, HEAD_PAD, t), BF16),
      compiler_params=_params("parallel", "parallel"),
      name="q_up",
  )(cqn, w_t, g_t, cs_t)


def _rope4(t4, cs, g4):
  t = t4 * cs * g4
  r = t + pltpu.roll(t, QK_ROPE, axis=1)
  lane = lax.broadcasted_iota(jnp.int32, r.shape, 1)
  return jnp.where(lane < QK_ROPE, r, 0.0)


def _kv_up_kernel(c_ref, kr4_ref, wk_ref, wv_ref, g_ref, cs_ref, k_ref, vt_ref):
  c = c_ref[0].astype(BF16)
  kr4 = kr4_ref[0]
  g = g_ref[...]
  kr = _rope4(kr4, cs_ref[...], g[:, LANES:])
  ssr = 0.5 * jnp.sum(kr4 * kr4, axis=-1, keepdims=True)
  for h0 in range(0, N_HEADS, 2):
    y = _mm(c, wk_ref[:, h0 * QK_NOPE:(h0 + 2) * QK_NOPE])
    for d in range(2):
      kn = y[:, d * QK_NOPE:(d + 1) * QK_NOPE]
      ss = jnp.sum(kn * kn, axis=-1, keepdims=True) + ssr
      inv = lax.rsqrt(ss * (1.0 / QK_HEAD) + EPS)
      k_ref[0, h0 + d, :, :LANES] = (kn * inv * g[:, :LANES]).astype(BF16)
      k_ref[0, h0 + d, :, LANES:] = (kr * inv).astype(BF16)
  vt = lax.dot_general(wv_ref[...], c, _NT, preferred_element_type=F32)
  for h in range(N_HEADS):
    vt_ref[0, h] = vt[h * V_HEAD:(h + 1) * V_HEAD].astype(BF16)


def _kv_up(ckv, kr4, wk, wv_t, g, cs, tm):
  b, t, r = ckv.shape
  return pl.pallas_call(
      _kv_up_kernel,
      grid=(b, t // tm),
      in_specs=[pl.BlockSpec((1, tm, r), lambda i, j: (i, j, 0)),
                pl.BlockSpec((1, tm, LANES), lambda i, j: (i, j, 0)),
                _const_spec(wk.shape), _const_spec(wv_t.shape), _const_spec(g.shape),
                pl.BlockSpec((tm, LANES), lambda i, j: (j, 0))],
      out_specs=(pl.BlockSpec((1, N_HEADS, tm, HEAD_PAD), lambda i, j: (i, 0, j, 0)),
                 pl.BlockSpec((1, N_HEADS, V_HEAD, tm), lambda i, j: (i, 0, 0, j))),
      out_shape=(jax.ShapeDtypeStruct((b, N_HEADS, t, HEAD_PAD), BF16),
                 jax.ShapeDtypeStruct((b, N_HEADS, V_HEAD, t), BF16)),
      compiler_params=_params("parallel", "parallel"),
      name="kv_up",
  )(ckv, kr4, wk, wv_t, g, cs)


def _attn_kernel(q_ref, k_ref, vt_ref, o_ref, acc_sc, ml_sc, s0_sc, s1_sc,
                 *, hp, tq, tk, n_valid, q_pos0, k_pos0):
  row0 = q_pos0 + pl.program_id(2) * tq
  keys_all = jnp.clip(((row0 >> CHUNK_SHIFT) + 1) * CHUNK - k_pos0, 0, n_valid)
  keys_any = jnp.clip((((row0 + tq - 1) >> CHUNK_SHIFT) + 1) * CHUNK - k_pos0, 0, n_valid)
  last = (keys_any + tk - 1) // tk - 1
  n_open = jnp.minimum(keys_all // tk, last)
  q_chunk = (row0 + lax.broadcasted_iota(jnp.int32, (1, tq), 1)) >> CHUNK_SHIFT
  acc_sc[...] = jnp.zeros(acc_sc.shape, F32)
  for h in range(hp):
    ml_sc[2 * h] = jnp.full((1, tq), NEG_INF, F32)
    ml_sc[2 * h + 1] = jnp.zeros((1, tq), F32)

  def scores(j, s_sc):
    ks = pl.multiple_of(j * tk, tk)
    for h in range(hp):
      s_sc[h] = _mm(k_ref[0, h, pl.ds(ks, tk), :], q_ref[0, h])

  def tile(j, s_cur, s_next, masked):
    if s_next is not None:
      scores(j + 1, s_next)
    ks = pl.multiple_of(j * tk, tk)
    if masked:
      k_idx = ks + lax.broadcasted_iota(jnp.int32, (tk, 1), 0)
      visible = (((k_pos0 + k_idx) >> CHUNK_SHIFT) <= q_chunk) & (k_idx < n_valid)
    ps, alphas = [], []
    for h in range(hp):
      m, l = ml_sc[2 * h], ml_sc[2 * h + 1]
      st = s_cur[h]
      if masked:
        st = jnp.where(visible, st, NEG_INF)
      m_new = jnp.maximum(m, jnp.max(st, axis=0, keepdims=True))
      alpha = jnp.exp2(m - m_new)
      p = jnp.exp2(st - m_new)
      ml_sc[2 * h] = m_new
      ml_sc[2 * h + 1] = alpha * l + jnp.sum(p, axis=0, keepdims=True)
      ps.append(p.astype(BF16))
      alphas.append(alpha)
    for h in range(hp):
      acc_sc[h] = alphas[h] * acc_sc[h] + _mm(vt_ref[0, h, :, pl.ds(ks, tk)], ps[h])

  def step(j, masked, prefetch):
    @pl.when((j & 1) == 0)
    def _():
      tile(j, s0_sc, s1_sc if prefetch else None, masked)

    @pl.when((j & 1) == 1)
    def _():
      tile(j, s1_sc, s0_sc if prefetch else None, masked)

  def loop_body(j, carry, masked):
    step(j, masked, True)
    return carry

  scores(0, s0_sc)
  lax.fori_loop(0, n_open, functools.partial(loop_body, masked=False), 0)
  lax.fori_loop(n_open, last, functools.partial(loop_body, masked=True), 0)
  step(last, True, False)
  for h in range(hp):
    o = acc_sc[h] * (1.0 / ml_sc[2 * h + 1])
    o_ref[0, :, h * V_HEAD:(h + 1) * V_HEAD] = o.T.astype(BF16)


def _attn(q_t, k, v_t, n_valid, q_pos0, k_pos0, hp, tq, tk):
  b, h, _, t = q_t.shape
  n_keys = k.shape[2]
  assert q_pos0 >= k_pos0 and n_keys % tk == 0 and t % tq == 0 and h % hp == 0
  return pl.pallas_call(
      functools.partial(_attn_kernel, hp=hp, tq=tq, tk=tk, n_valid=n_valid,
                        q_pos0=q_pos0, k_pos0=k_pos0),
      grid=(b, h // hp, t // tq),
      in_specs=[pl.BlockSpec((1, hp, HEAD_PAD, tq), lambda i, j, l: (i, j, 0, l)),
                pl.BlockSpec((1, hp, n_keys, HEAD_PAD), lambda i, j, l: (i, j, 0, 0)),
                pl.BlockSpec((1, hp, V_HEAD, n_keys), lambda i, j, l: (i, j, 0, 0))],
      out_specs=pl.BlockSpec((1, tq, hp * V_HEAD), lambda i, j, l: (i, l, j)),
      out_shape=jax.ShapeDtypeStruct((b, t, h * V_HEAD), BF16),
      scratch_shapes=[pltpu.VMEM((hp, V_HEAD, tq), F32), pltpu.VMEM((2 * hp, 1, tq), F32),
                      pltpu.VMEM((hp, tk, tq), F32), pltpu.VMEM((hp, tk, tq), F32)],
      compiler_params=_params("parallel", "parallel", "arbitrary"),
      name="attn",
  )(q_t, k, v_t)


def _out_kernel(x_ref, att_ref, hs_ref, ga_ref, gb_ref, wpa_ref, wpr_ref, wo_ref, g2_ref,
                wg_ref, wu_ref, wd_ref, y_ref, *, ff_chunk):
  o_att = _mm(att_ref[0], wpa_ref[...])
  o_rnn = _mm(hs_ref[0], wpr_ref[...])
  merged = ga_ref[0].astype(F32) * o_att + gb_ref[0].astype(F32) * o_rnn
  x1 = x_ref[0] + _mm(merged.astype(BF16), wo_ref[...])
  xn = _rms(x1, g2_ref[...]).astype(BF16)
  acc = x1
  for c0 in range(0, wg_ref.shape[1], ff_chunk):
    g = _mm(xn, wg_ref[:, c0:c0 + ff_chunk])
    u = _mm(xn, wu_ref[:, c0:c0 + ff_chunk])
    acc = acc + _mm((g * _sigmoid(g) * u).astype(BF16), wd_ref[c0:c0 + ff_chunk, :])
  y_ref[0] = acc


def _out(x, att, hs, ga, gb, wpa, wpr, wo, g2, wg, wu, wd, tm):
  b, t, d = x.shape
  d_ff = wg.shape[1]
  ff_chunk = 2 * LANES if d_ff % (2 * LANES) == 0 else d_ff
  row = lambda n: pl.BlockSpec((1, tm, n), lambda i, j: (i, j, 0))
  consts = [wpa, wpr, wo, g2, wg, wu, wd]
  return pl.pallas_call(
      functools.partial(_out_kernel, ff_chunk=ff_chunk),
      grid=(b, t // tm),
      in_specs=[row(d), row(att.shape[2]), row(hs.shape[2]), row(d), row(d)]
      + [_const_spec(a.shape) for a in consts],
      out_specs=row(d),
      out_shape=jax.ShapeDtypeStruct((b, t, d), F32),
      compiler_params=_params("parallel", "parallel"),
      name="out",
  )(x, att, hs, ga, gb, *consts)


def _quad(a):
  x1, x2 = a[..., :HALF], a[..., HALF:]
  return jnp.concatenate([x1, x2, -x2, x1], axis=-1)


def _gain_row(g):
  g1, g2 = g[QK_NOPE:QK_NOPE + HALF], g[QK_NOPE + HALF:]
  return jnp.concatenate([g[:QK_NOPE], g1, g2, g2, g1])[None, :]


def _rope_table(pos0, n):
  inv_freq = jnp.exp(-math.log(ROPE_THETA) * jnp.arange(HALF, dtype=F32) / HALF)
  ang = (pos0 + jnp.arange(n, dtype=jnp.int32)).astype(F32)[:, None] * inv_freq[None, :]
  c, s = jnp.cos(ang), jnp.sin(ang)
  return jnp.concatenate([c, c, s, s], axis=-1)


def _prep_layer(l, norm1_g, w_in, q_norm_g, w_uq, kv_norm_g, w_ukv, qk_q_g, qk_k_g,
                conv_w, conv_b, w_rg_a, b_rg_a, w_rg_x, b_rg_x, lru_lambda,
                w_proj_attn, w_proj_rnn, w_out, norm2_g, w_ffn_gate, w_ffn_up, w_ffn_down):
  q_lora, kv_lora = q_norm_g.shape[1], kv_norm_g.shape[1]
  d_rnn, d_model = conv_b.shape[1], norm1_g.shape[1]
  o_kr = q_lora + kv_lora
  o_x = o_kr + QK_ROPE
  w = w_in[l]
  w_in4 = jnp.concatenate([w[:, :o_kr], _quad(w[:, o_kr:o_x]), w[:, o_x:]], axis=1).astype(BF16)
  wq = w_uq[l].reshape(q_lora, N_HEADS, QK_HEAD)
  wq4 = jnp.concatenate([wq[..., :QK_NOPE], _quad(wq[..., QK_NOPE:])], axis=-1)
  wq_t = wq4.reshape(q_lora, N_HEADS * HEAD_PAD).T.astype(BF16)
  wkv = w_ukv[l].reshape(kv_lora, N_HEADS, QK_NOPE + V_HEAD)
  wk = wkv[..., :QK_NOPE].reshape(kv_lora, N_HEADS * QK_NOPE).astype(BF16)
  wv_t = wkv[..., QK_NOPE:].reshape(kv_lora, N_HEADS * V_HEAD).T.astype(BF16)
  w_gate = jnp.concatenate([w_rg_a[l], w_rg_x[l]], axis=-1).astype(BF16)
  row = lambda a: a[l][None, :]
  tile8 = lambda a: a[l].reshape(SUBLANES, LANES)
  return dict(
      widths=(q_lora, kv_lora, d_rnn, d_model),
      g1=row(norm1_g), w_in=w_in4, gq=row(q_norm_g), gkv=row(kv_norm_g),
      wq_t=wq_t, gq_t=jnp.broadcast_to(_gain_row(qk_q_g[l]).T, (HEAD_PAD, LANES)),
      wk=wk, wv_t=wv_t, gkh=_gain_row(qk_k_g[l]),
      cw=conv_w[l].reshape(CONV_W, SUBLANES, LANES), cb=tile8(conv_b), w_gate=w_gate,
      ba=tile8(b_rg_a), bx=tile8(b_rg_x), lam=tile8(lru_lambda),
      wpa=w_proj_attn[l].astype(BF16), wpr=w_proj_rnn[l].astype(BF16), wo=w_out[l].astype(BF16),
      g2=row(norm2_g), wg=w_ffn_gate[l].astype(BF16), wu=w_ffn_up[l].astype(BF16),
      wd=w_ffn_down[l].astype(BF16))


def _layer(x, pos0, past, p):
  b, t, _ = x.shape
  d_rnn = p["widths"][2]
  assert t >= SUBLANES and t % SUBLANES == 0 and d_rnn == SUBLANES * LANES
  if past is None:
    hist = jnp.zeros((b, HIST_ROWS, LANES), F32)
    h0 = jnp.zeros((b, SUBLANES, LANES), F32)
  else:
    past_ckv, past_krope, conv_hist, state_h = past
    hist = conv_hist.reshape(b, HIST_ROWS, LANES)
    h0 = state_h.reshape(b, SUBLANES, LANES)
  tm = _tile(t, 512)
  cqn, ckv, kr4, kr, xtail, ga, gb, hs, hl = _in_rnn(x, hist, h0, p, tm)

  t_pad = -(-t // LANES) * LANES
  q_t = _q_up(_pad_axis(cqn, 1, t_pad), p["wq_t"], p["gq_t"], _rope_table(pos0, t_pad).T,
              _tile(t_pad, 512, LANES))
  if past is None:
    ckv_all, kr4_all, k_pos0 = ckv, kr4, pos0
  else:
    k_pos0 = pos0 - past_ckv.shape[1]
    ckv_all = jnp.concatenate([past_ckv, ckv], axis=1)
    kr4_all = jnp.concatenate([_quad(past_krope), kr4], axis=1)
  n_keys = ckv_all.shape[1]
  n_pad = -(-n_keys // LANES) * LANES
  k, v_t = _kv_up(_pad_axis(ckv_all, 1, n_pad), _pad_axis(kr4_all, 1, n_pad), p["wk"], p["wv_t"],
                  p["gkh"], _rope_table(k_pos0, n_pad), _tile(n_pad, 512, LANES))
  att = _attn(q_t, k, v_t, n_keys, pos0, k_pos0, hp=ATTN_HEADS_PER_STEP,
              tq=_tile(t_pad, ATTN_TQ, LANES), tk=_tile(n_pad, ATTN_TK, LANES))[:, :t]

  y = _out(x, att, hs, ga, gb, p["wpa"], p["wpr"], p["wo"], p["g2"], p["wg"], p["wu"], p["wd"], tm)
  return y, ckv, kr, xtail[:, SUBLANES - (CONV_W - 1):, :], hl.reshape(b, d_rnn)


def kernel(x_prompt, x_sample, cache_ckv, cache_krope, state_conv, state_h,
           norm1_g, w_in, q_norm_g, w_uq, kv_norm_g, w_ukv, qk_q_g, qk_k_g,
           conv_w, conv_b, w_rg_a, b_rg_a, w_rg_x, b_rg_x, lru_lambda,
           w_proj_attn, w_proj_rnn, w_out, norm2_g, w_ffn_gate, w_ffn_up, w_ffn_down):
  depth = w_in.shape[0]
  past_len = cache_ckv.shape[2]
  y_p, y_s = x_prompt, x_sample
  outs_p, outs_s = [], []
  for l in range(depth):
    p = _prep_layer(l, norm1_g, w_in, q_norm_g, w_uq, kv_norm_g, w_ukv, qk_q_g, qk_k_g,
                    conv_w, conv_b, w_rg_a, b_rg_a, w_rg_x, b_rg_x, lru_lambda,
                    w_proj_attn, w_proj_rnn, w_out, norm2_g, w_ffn_gate, w_ffn_up, w_ffn_down)
    y_p, *new_p = _layer(y_p, 0, None, p)
    past = (cache_ckv[l], cache_krope[l], state_conv[l], state_h[l])
    y_s, *new_s = _layer(y_s, past_len, past, p)
    outs_p.append(new_p)
    outs_s.append(new_s)
  stack = lambda outs, i: jnp.stack([o[i] for o in outs])
  return (y_p, y_s,
          stack(outs_p, 0), stack(outs_p, 1), stack(outs_p, 2), stack(outs_p, 3),
          stack(outs_s, 0), stack(outs_s, 1), stack(outs_s, 2), stack(outs_s, 3))
```

```python
import functools
import math

import jax
import jax.numpy as jnp
from jax import lax
from jax.experimental import pallas as pl
from jax.experimental.pallas import tpu as pltpu

CHUNK = 64
N_HEADS = 8
QK_NOPE = 128
QK_ROPE = 64
QK_HEAD = QK_NOPE + QK_ROPE
V_HEAD = 128
ROPE_THETA = 10000.0
CONV_W = 4
LRU_C = 8.0
RNN_BLOCKS = 8
EPS = 1e-6
NEG_INF = -1e30
SCALE = QK_HEAD ** -0.5
LOG2E = 1.4426950408889634
F32_MIN_NORMAL = 1.1754943508222875e-38

LANES = 128
SUBLANES = 8
BF16_ROWS = 16
VMEM_LIMIT_BYTES = 58 * 1024 * 1024

HEAD_PAD = 2 * LANES
HALF = QK_ROPE // 2
CHUNK_SHIFT = CHUNK.bit_length() - 1
assert 1 << CHUNK_SHIFT == CHUNK

ATTN_HEADS_PER_STEP = 4
ATTN_TQ = 512
ATTN_TK = 512
SCORE_LEAD = 1
MAX_STATIC_SHIFT = 48.0
SHIFT_MARGIN = 1.02

F32 = jnp.float32
BF16 = jnp.bfloat16


def _tile(n, target, unit=BF16_ROWS):
  if n <= target:
    return n
  for t in range(target - target % unit, 0, -unit):
    if n % t == 0:
      return t
  return n


def _pad_axis(a, axis, n):
  if a.shape[axis] == n:
    return a
  widths = [(0, 0)] * a.ndim
  widths[axis] = (0, n - a.shape[axis])
  return jnp.pad(a, widths)


def _const_spec(shape):
  nd = len(shape)
  return pl.BlockSpec(shape, lambda *_: (0,) * nd, pipeline_mode=pl.Buffered(1))


def _params(*sem):
  return pltpu.CompilerParams(dimension_semantics=sem, vmem_limit_bytes=VMEM_LIMIT_BYTES)


def _rms(x, g):
  return x * lax.rsqrt(jnp.mean(x * x, axis=-1, keepdims=True) + EPS) * g


def _sigmoid(x):
  return 0.5 * jnp.tanh(0.5 * x) + 0.5


def _sqrt_nonneg(x):
  return x * lax.rsqrt(jnp.maximum(x, F32_MIN_NORMAL))


def _mm(a, b):
  return jnp.dot(a, b, preferred_element_type=F32)


HIST_ROWS = (CONV_W - 1) * SUBLANES
RNN_PARTS = 4


def _in_rnn_kernel(x_ref, g1_ref, w_ref, gq_ref, gkv_ref, hist_ref, h0_ref, cw_ref, cb_ref,
                   wgate_ref, ba_ref, bx_ref, lam_ref,
                   cqn_ref, ckv_ref, kr4_ref, kr_ref, xtail_ref, ga_ref, gb_ref, hs_ref, hl_ref,
                   xs, xcs, ras, ias, hcar, *, widths, tm):
  q_lora, kv_lora, d_rnn, d_model = widths
  o_kv, o_kr = q_lora, q_lora + kv_lora
  o_x = o_kr + LANES
  o_ga = o_x + d_rnn
  o_gb = o_ga + d_model
  rows = tm * SUBLANES
  blk = d_rnn // RNN_BLOCKS
  assert blk == LANES and RNN_BLOCKS == SUBLANES

  @pl.when(pl.program_id(1) == 0)
  def _():
    xs[0:HIST_ROWS, :] = hist_ref[0]
    hcar[...] = h0_ref[0]

  xn = _rms(x_ref[0], g1_ref[...]).astype(BF16)
  seg = lambda off, n: _mm(xn, w_ref[:, off:off + n])
  strided = lambda n: pl.ds(n, tm, stride=SUBLANES)
  tmajor = lambda v: v.reshape(tm, SUBLANES, LANES)

  xr = seg(o_x, d_rnn)
  xtail_ref[0] = xr[tm - SUBLANES:, :]
  for n in range(RNN_BLOCKS):
    xs[pl.ds(HIST_ROWS + n, tm, stride=SUBLANES), :] = xr[:, n * blk:(n + 1) * blk]
  xc = cb_ref[...][None] + tmajor(xs[0:rows, :]) * cw_ref[0][None]
  for j in range(1, CONV_W):
    xc = xc + tmajor(xs[j * SUBLANES:j * SUBLANES + rows, :]) * cw_ref[j][None]
  xcs[...] = xc.reshape(rows, LANES)

  z = -lam_ref[...]
  softplus = jnp.maximum(z, 0.0) + jnp.log1p(jnp.exp(-jnp.abs(z)))
  half_rate = (-0.5 * LRU_C) * softplus

  n_parts = RNN_PARTS if tm % (RNN_PARTS * SUBLANES) == 0 else 1
  pt = tm // n_parts

  def rnn_part(p, h):
    span = slice(p * pt * SUBLANES, (p + 1) * pt * SUBLANES)
    part = lambda v: v.reshape(pt, SUBLANES, LANES)
    for n in range(RNN_BLOCKS):
      rows_n = pl.ds(p * pt * SUBLANES + n, pt, stride=SUBLANES)
      gates = _mm(xcs[rows_n, :].astype(BF16), wgate_ref[n])
      ras[rows_n, :] = gates[:, :blk]
      ias[rows_n, :] = gates[:, blk:]
    r = _sigmoid(part(ras[span, :]) + ba_ref[...][None])
    i = _sigmoid(part(ias[span, :]) + bx_ref[...][None])
    th = jnp.tanh(half_rate[None] * r)
    rden = 1.0 / (1.0 - th)
    a = (1.0 + th) * rden
    ras[span, :] = a.reshape(pt * SUBLANES, LANES)
    u = _sqrt_nonneg((th * (-2.0 * rden)) * (1.0 + a)) * (i * part(xcs[span, :]))
    ias[span, :] = u.reshape(pt * SUBLANES, LANES)
    for t in range(p * pt, (p + 1) * pt):
      step = slice(t * SUBLANES, (t + 1) * SUBLANES)
      h = ras[step, :] * h + ias[step, :]
      ias[step, :] = h
    return h

  def q_cols():
    cqn_ref[0] = _rms(seg(0, q_lora), gq_ref[...]).astype(BF16)

  def kv_cols():
    ckv_ref[0] = _rms(seg(o_kv, kv_lora), gkv_ref[...])
    kr4 = seg(o_kr, LANES)
    kr4_ref[0] = kr4
    kr_ref[0] = kr4[:, :QK_ROPE]

  def ga_cols():
    ga_ref[0] = _sigmoid(seg(o_ga, d_model)).astype(BF16)

  def gb_cols():
    gb_ref[0] = _sigmoid(seg(o_gb, d_model)).astype(BF16)

  column_groups = [q_cols, kv_cols, ga_cols, gb_cols]
  h = hcar[...]
  for p in range(n_parts):
    h = rnn_part(p, h)
    for cols in column_groups[p::n_parts]:
      cols()
  hcar[...] = h
  hl_ref[0] = h
  for n in range(RNN_BLOCKS):
    hs_ref[0, :, n * blk:(n + 1) * blk] = ias[strided(n), :].astype(BF16)
  xs[0:HIST_ROWS, :] = xs[rows:rows + HIST_ROWS, :]


def _in_rnn(x, hist, h0, p, tm):
  b, t, d = x.shape
  widths = p["widths"]
  q_lora, kv_lora, d_rnn, d_model = widths
  row = lambda n: pl.BlockSpec((1, tm, n), lambda i, j: (i, j, 0))
  per_batch = lambda r: pl.BlockSpec((1, r, LANES), lambda i, j: (i, 0, 0))
  consts = [p["g1"], p["w_in"], p["gq"], p["gkv"]]
  rnn_consts = [p["cw"], p["cb"], p["w_gate"], p["ba"], p["bx"], p["lam"]]
  out_shape = (
      jax.ShapeDtypeStruct((b, t, q_lora), BF16),
      jax.ShapeDtypeStruct((b, t, kv_lora), F32),
      jax.ShapeDtypeStruct((b, t, LANES), F32),
      jax.ShapeDtypeStruct((b, t, QK_ROPE), F32),
      jax.ShapeDtypeStruct((b, SUBLANES, d_rnn), F32),
      jax.ShapeDtypeStruct((b, t, d_model), BF16),
      jax.ShapeDtypeStruct((b, t, d_model), BF16),
      jax.ShapeDtypeStruct((b, t, d_rnn), BF16),
      jax.ShapeDtypeStruct((b, SUBLANES, LANES), F32),
  )
  out_specs = (row(q_lora), row(kv_lora), row(LANES), row(QK_ROPE),
               pl.BlockSpec((1, SUBLANES, d_rnn), lambda i, j: (i, 0, 0)),
               row(d_model), row(d_model), row(d_rnn), per_batch(SUBLANES))
  rows = tm * SUBLANES
  return pl.pallas_call(
      functools.partial(_in_rnn_kernel, widths=widths, tm=tm),
      grid=(b, t // tm),
      in_specs=[row(d)] + [_const_spec(a.shape) for a in consts]
      + [per_batch(HIST_ROWS), per_batch(SUBLANES)] + [_const_spec(a.shape) for a in rnn_consts],
      out_specs=out_specs,
      out_shape=out_shape,
      scratch_shapes=[pltpu.VMEM((rows + HIST_ROWS, LANES), F32), pltpu.VMEM((rows, LANES), F32),
                      pltpu.VMEM((rows, LANES), F32), pltpu.VMEM((rows, LANES), F32),
                      pltpu.VMEM((SUBLANES, LANES), F32)],
      compiler_params=_params("parallel", "arbitrary"),
      name="in_rnn",
  )(x, *consts, hist, h0, *rnn_consts)


_NT = (((1,), (1,)), ((), ()))
SHIFT_COL = QK_HEAD


def _q_up_kernel(shift_ref, c_ref, w_ref, g_ref, cs_ref, q_ref, *, heads_per_dot):
  c = c_ref[0]
  tm = c.shape[0]
  g = jnp.concatenate([g_ref[...]] * (tm // LANES), axis=1)
  gn = g[:LANES]
  rot = cs_ref[...] * g[LANES:]
  rows = heads_per_dot * HEAD_PAD
  pad_row = lax.broadcasted_iota(jnp.int32, (HEAD_PAD - SHIFT_COL, tm), 0)
  pad = jnp.where(pad_row == 0, -shift_ref[0], 0.0).astype(BF16)
  for h0 in range(0, N_HEADS, heads_per_dot):
    y_all = lax.dot_general(w_ref[h0 * HEAD_PAD:h0 * HEAD_PAD + rows, :], c, _NT,
                            preferred_element_type=F32)
    for d in range(heads_per_dot):
      h = h0 + d
      yn = y_all[d * HEAD_PAD:d * HEAD_PAD + LANES]
      yr = y_all[d * HEAD_PAD + LANES:(d + 1) * HEAD_PAD]
      ss = jnp.sum(yn * yn, axis=0, keepdims=True) + 0.5 * jnp.sum(yr * yr, axis=0, keepdims=True)
      inv = lax.rsqrt(ss * (1.0 / QK_HEAD) + EPS) * (SCALE * LOG2E)
      t = yr * rot
      q_ref[0, h, :LANES, :] = (yn * gn * inv).astype(BF16)
      q_ref[0, h, LANES:LANES + QK_ROPE, :] = ((t[:QK_ROPE] + t[QK_ROPE:]) * inv).astype(BF16)
      q_ref[0, h, SHIFT_COL:, :] = pad


def _q_up(shift, cqn, w_t, g_t, cs_t, tm):
  b, t, r = cqn.shape
  return pl.pallas_call(
      functools.partial(_q_up_kernel, heads_per_dot=2),
      grid=(b, t // tm),
      in_specs=[pl.BlockSpec(memory_space=pltpu.SMEM),
                pl.BlockSpec((1, tm, r), lambda i, j: (i, j, 0)),
                _const_spec(w_t.shape), _const_spec(g_t.shape),
                pl.BlockSpec((LANES, tm), lambda i, j: (0, j))],
      out_specs=pl.BlockSpec((1, N_HEADS, HEAD_PAD, tm), lambda i, j: (i, 0, 0, j)),
      out_shape=jax.ShapeDtypeStruct((b, N_HEADS, HEAD_PAD, t), BF16),
      compiler_params=_params("parallel", "parallel"),
      name="q_up",
  )(shift, cqn, w_t, g_t, cs_t)


def _rope4(t4, cs, g4):
  t = t4 * cs * g4
  r = t + pltpu.roll(t, QK_ROPE, axis=1)
  lane = lax.broadcasted_iota(jnp.int32, r.shape, 1)
  return jnp.where(lane < QK_ROPE, r, 0.0)


def _kv_up_kernel(c_ref, kr4_ref, wk_ref, wv_ref, g_ref, cs_ref, k_ref, vt_ref):
  c = c_ref[0].astype(BF16)
  kr4 = kr4_ref[0]
  g = g_ref[...]
  kr = _rope4(kr4, cs_ref[...], g[:, LANES:])
  ssr = 0.5 * jnp.sum(kr4 * kr4, axis=-1, keepdims=True)
  lane = lax.broadcasted_iota(jnp.int32, (1, LANES), 1)
  one_hot = jnp.where(lane == SHIFT_COL - LANES, 1.0, 0.0)
  for h0 in range(0, N_HEADS, 2):
    y = _mm(c, wk_ref[:, h0 * QK_NOPE:(h0 + 2) * QK_NOPE])
    for d in range(2):
      kn = y[:, d * QK_NOPE:(d + 1) * QK_NOPE]
      ss = jnp.sum(kn * kn, axis=-1, keepdims=True) + ssr
      inv = lax.rsqrt(ss * (1.0 / QK_HEAD) + EPS)
      k_ref[0, h0 + d, :, :LANES] = (kn * inv * g[:, :LANES]).astype(BF16)
      k_ref[0, h0 + d, :, LANES:] = (kr * inv + one_hot).astype(BF16)
  vt = lax.dot_general(wv_ref[...], c, _NT, preferred_element_type=F32)
  for h in range(N_HEADS):
    vt_ref[0, h] = vt[h * V_HEAD:(h + 1) * V_HEAD].astype(BF16)


def _kv_up(ckv, kr4, wk, wv_t, g, cs, tm):
  b, t, r = ckv.shape
  return pl.pallas_call(
      _kv_up_kernel,
      grid=(b, t // tm),
      in_specs=[pl.BlockSpec((1, tm, r), lambda i, j: (i, j, 0)),
                pl.BlockSpec((1, tm, LANES), lambda i, j: (i, j, 0)),
                _const_spec(wk.shape), _const_spec(wv_t.shape), _const_spec(g.shape),
                pl.BlockSpec((tm, LANES), lambda i, j: (j, 0))],
      out_specs=(pl.BlockSpec((1, N_HEADS, tm, HEAD_PAD), lambda i, j: (i, 0, j, 0)),
                 pl.BlockSpec((1, N_HEADS, V_HEAD, tm), lambda i, j: (i, 0, 0, j))),
      out_shape=(jax.ShapeDtypeStruct((b, N_HEADS, t, HEAD_PAD), BF16),
                 jax.ShapeDtypeStruct((b, N_HEADS, V_HEAD, t), BF16)),
      compiler_params=_params("parallel", "parallel"),
      name="kv_up",
  )(ckv, kr4, wk, wv_t, g, cs)


def _attn_kernel(q_ref, k_ref, vt_ref, o_ref, acc_sc, ml_sc, s0_sc, s1_sc,
                 *, hp, tq, tk, n_valid, q_pos0, k_pos0, online):
  row0 = q_pos0 + pl.program_id(2) * tq
  keys_all = jnp.clip(((row0 >> CHUNK_SHIFT) + 1) * CHUNK - k_pos0, 0, n_valid)
  keys_any = jnp.clip((((row0 + tq - 1) >> CHUNK_SHIFT) + 1) * CHUNK - k_pos0, 0, n_valid)
  last = (keys_any + tk - 1) // tk - 1
  n_open = jnp.minimum(keys_all // tk, last)
  q_chunk = (row0 + lax.broadcasted_iota(jnp.int32, (1, tq), 1)) >> CHUNK_SHIFT
  acc_sc[...] = jnp.zeros(acc_sc.shape, F32)
  for h in range(hp):
    ml_sc[2 * h] = jnp.full((1, tq), NEG_INF, F32)
    ml_sc[2 * h + 1] = jnp.zeros((1, tq), F32)

  def score(j, s_sc, h):
    ks = pl.multiple_of(j * tk, tk)
    s_sc[h] = _mm(k_ref[0, h, pl.ds(ks, tk), :], q_ref[0, h])

  def scores(j, s_sc):
    for h in range(hp):
      score(j, s_sc, h)

  def tile(j, s_cur, s_next, masked):
    lead = min(SCORE_LEAD, hp)
    if s_next is not None:
      for h in range(lead):
        score(j + 1, s_next, h)
    ks = pl.multiple_of(j * tk, tk)
    if masked:
      k_idx = ks + lax.broadcasted_iota(jnp.int32, (tk, 1), 0)
      k_chunk = jnp.where(k_idx < n_valid, (k_pos0 + k_idx) >> CHUNK_SHIFT, jnp.iinfo(jnp.int32).max)
      visible = k_chunk <= q_chunk
    for h in range(hp):
      m, l = ml_sc[2 * h], ml_sc[2 * h + 1]
      st = s_cur[h]
      if masked:
        st = jnp.where(visible, st, NEG_INF)
      if online:
        m_new = jnp.maximum(m, jnp.max(st, axis=0, keepdims=True))
        alpha = jnp.exp2(m - m_new)
        p = jnp.exp2(st - m_new)
        ml_sc[2 * h] = m_new
        ml_sc[2 * h + 1] = alpha * l + jnp.sum(p, axis=0, keepdims=True)
      else:
        p = jnp.exp2(st)
        ml_sc[2 * h + 1] = l + jnp.sum(p, axis=0, keepdims=True)
      pv = _mm(vt_ref[0, h, :, pl.ds(ks, tk)], p.astype(BF16))
      acc_sc[h] = alpha * acc_sc[h] + pv if online else acc_sc[h] + pv
      if s_next is not None and h + lead < hp:
        score(j + 1, s_next, h + lead)

  def step(j, masked, prefetch):
    @pl.when((j & 1) == 0)
    def _():
      tile(j, s0_sc, s1_sc if prefetch else None, masked)

    @pl.when((j & 1) == 1)
    def _():
      tile(j, s1_sc, s0_sc if prefetch else None, masked)

  def loop_body(j, carry, masked):
    step(j, masked, True)
    return carry

  scores(0, s0_sc)
  lax.fori_loop(0, n_open, functools.partial(loop_body, masked=False), 0)
  lax.fori_loop(n_open, last, functools.partial(loop_body, masked=True), 0)
  step(last, True, False)
  for h in range(hp):
    o = acc_sc[h] * (1.0 / ml_sc[2 * h + 1])
    o_ref[0, :, h * V_HEAD:(h + 1) * V_HEAD] = o.T.astype(BF16)


def _attn(q_t, k, v_t, n_valid, q_pos0, k_pos0, hp, tq, tk, online):
  b, h, _, t = q_t.shape
  n_keys = k.shape[2]
  assert q_pos0 >= k_pos0 and n_keys % tk == 0 and t % tq == 0 and h % hp == 0
  return pl.pallas_call(
      functools.partial(_attn_kernel, hp=hp, tq=tq, tk=tk, n_valid=n_valid,
                        q_pos0=q_pos0, k_pos0=k_pos0, online=online),
      grid=(b, h // hp, t // tq),
      in_specs=[pl.BlockSpec((1, hp, HEAD_PAD, tq), lambda i, j, l: (i, j, 0, l)),
                pl.BlockSpec((1, hp, n_keys, HEAD_PAD), lambda i, j, l: (i, j, 0, 0)),
                pl.BlockSpec((1, hp, V_HEAD, n_keys), lambda i, j, l: (i, j, 0, 0))],
      out_specs=pl.BlockSpec((1, tq, hp * V_HEAD), lambda i, j, l: (i, l, j)),
      out_shape=jax.ShapeDtypeStruct((b, t, h * V_HEAD), BF16),
      scratch_shapes=[pltpu.VMEM((hp, V_HEAD, tq), F32), pltpu.VMEM((2 * hp, 1, tq), F32),
                      pltpu.VMEM((hp, tk, tq), F32), pltpu.VMEM((hp, tk, tq), F32)],
      compiler_params=_params("parallel", "parallel", "arbitrary"),
      name="attn",
  )(q_t, k, v_t)


def _out_kernel(x_ref, att_ref, hs_ref, ga_ref, gb_ref, wpa_ref, wpr_ref, wo_ref, g2_ref,
                wg_ref, wu_ref, wd_ref, y_ref, *, ff_chunk):
  o_att = _mm(att_ref[0], wpa_ref[...])
  o_rnn = _mm(hs_ref[0], wpr_ref[...])
  merged = ga_ref[0].astype(F32) * o_att + gb_ref[0].astype(F32) * o_rnn
  x1 = x_ref[0] + _mm(merged.astype(BF16), wo_ref[...])
  xn = _rms(x1, g2_ref[...]).astype(BF16)
  acc = x1
  for c0 in range(0, wg_ref.shape[1], ff_chunk):
    g = _mm(xn, wg_ref[:, c0:c0 + ff_chunk])
    u = _mm(xn, wu_ref[:, c0:c0 + ff_chunk])
    acc = acc + _mm((g * _sigmoid(g) * u).astype(BF16), wd_ref[c0:c0 + ff_chunk, :])
  y_ref[0] = acc


def _out(x, att, hs, ga, gb, wpa, wpr, wo, g2, wg, wu, wd, tm):
  b, t, d = x.shape
  d_ff = wg.shape[1]
  ff_chunk = 2 * LANES if d_ff % (2 * LANES) == 0 else d_ff
  row = lambda n: pl.BlockSpec((1, tm, n), lambda i, j: (i, j, 0))
  consts = [wpa, wpr, wo, g2, wg, wu, wd]
  return pl.pallas_call(
      functools.partial(_out_kernel, ff_chunk=ff_chunk),
      grid=(b, t // tm),
      in_specs=[row(d), row(att.shape[2]), row(hs.shape[2]), row(d), row(d)]
      + [_const_spec(a.shape) for a in consts],
      out_specs=row(d),
      out_shape=jax.ShapeDtypeStruct((b, t, d), F32),
      compiler_params=_params("parallel", "parallel"),
      name="out",
  )(x, att, hs, ga, gb, *consts)


def _quad(a):
  x1, x2 = a[..., :HALF], a[..., HALF:]
  return jnp.concatenate([x1, x2, -x2, x1], axis=-1)


def _gain_row(g):
  g1, g2 = g[QK_NOPE:QK_NOPE + HALF], g[QK_NOPE + HALF:]
  return jnp.concatenate([g[:QK_NOPE], g1, g2, g2, g1])[None, :]


def _rope_table(pos0, n):
  inv_freq = jnp.exp(-math.log(ROPE_THETA) * jnp.arange(HALF, dtype=F32) / HALF)
  ang = (pos0 + jnp.arange(n, dtype=jnp.int32)).astype(F32)[:, None] * inv_freq[None, :]
  c, s = jnp.cos(ang), jnp.sin(ang)
  return jnp.concatenate([c, c, s, s], axis=-1)


def _prep_layer(l, norm1_g, w_in, q_norm_g, w_uq, kv_norm_g, w_ukv, qk_q_g, qk_k_g,
                conv_w, conv_b, w_rg_a, b_rg_a, w_rg_x, b_rg_x, lru_lambda,
                w_proj_attn, w_proj_rnn, w_out, norm2_g, w_ffn_gate, w_ffn_up, w_ffn_down):
  q_lora, kv_lora = q_norm_g.shape[1], kv_norm_g.shape[1]
  d_rnn, d_model = conv_b.shape[1], norm1_g.shape[1]
  o_kr = q_lora + kv_lora
  o_x = o_kr + QK_ROPE
  w = w_in[l]
  w_in4 = jnp.concatenate([w[:, :o_kr], _quad(w[:, o_kr:o_x]), w[:, o_x:]], axis=1).astype(BF16)
  wq = w_uq[l].reshape(q_lora, N_HEADS, QK_HEAD)
  wq4 = jnp.concatenate([wq[..., :QK_NOPE], _quad(wq[..., QK_NOPE:])], axis=-1)
  wq_t = wq4.reshape(q_lora, N_HEADS * HEAD_PAD).T.astype(BF16)
  wkv = w_ukv[l].reshape(kv_lora, N_HEADS, QK_NOPE + V_HEAD)
  wk = wkv[..., :QK_NOPE].reshape(kv_lora, N_HEADS * QK_NOPE).astype(BF16)
  wv_t = wkv[..., QK_NOPE:].reshape(kv_lora, N_HEADS * V_HEAD).T.astype(BF16)
  w_gate = jnp.concatenate([w_rg_a[l], w_rg_x[l]], axis=-1).astype(BF16)
  shift = (SHIFT_MARGIN * QK_HEAD * SCALE * LOG2E) * (
      jnp.max(jnp.abs(qk_q_g[l])) * jnp.max(jnp.abs(qk_k_g[l])))
  row = lambda a: a[l][None, :]
  tile8 = lambda a: a[l].reshape(SUBLANES, LANES)
  return dict(
      widths=(q_lora, kv_lora, d_rnn, d_model),
      g1=row(norm1_g), w_in=w_in4, gq=row(q_norm_g), gkv=row(kv_norm_g),
      wq_t=wq_t, gq_t=jnp.broadcast_to(_gain_row(qk_q_g[l]).T, (HEAD_PAD, LANES)),
      wk=wk, wv_t=wv_t, gkh=_gain_row(qk_k_g[l]), shift=shift.reshape(1).astype(F32),
      cw=conv_w[l].reshape(CONV_W, SUBLANES, LANES), cb=tile8(conv_b), w_gate=w_gate,
      ba=tile8(b_rg_a), bx=tile8(b_rg_x), lam=tile8(lru_lambda),
      wpa=w_proj_attn[l].astype(BF16), wpr=w_proj_rnn[l].astype(BF16), wo=w_out[l].astype(BF16),
      g2=row(norm2_g), wg=w_ffn_gate[l].astype(BF16), wu=w_ffn_up[l].astype(BF16),
      wd=w_ffn_down[l].astype(BF16))


def _layer(x, pos0, past, p):
  b, t, _ = x.shape
  d_rnn = p["widths"][2]
  assert t >= SUBLANES and t % SUBLANES == 0 and d_rnn == SUBLANES * LANES
  if past is None:
    hist = jnp.zeros((b, HIST_ROWS, LANES), F32)
    h0 = jnp.zeros((b, SUBLANES, LANES), F32)
  else:
    past_ckv, past_krope, conv_hist, state_h = past
    hist = conv_hist.reshape(b, HIST_ROWS, LANES)
    h0 = state_h.reshape(b, SUBLANES, LANES)
  tm = _tile(t, 512)
  cqn, ckv, kr4, kr, xtail, ga, gb, hs, hl = _in_rnn(x, hist, h0, p, tm)

  t_pad = -(-t // LANES) * LANES
  q_t = _q_up(p["shift"], _pad_axis(cqn, 1, t_pad), p["wq_t"], p["gq_t"],
              _rope_table(pos0, t_pad).T, _tile(t_pad, 512, LANES))
  if past is None:
    ckv_all, kr4_all, k_pos0 = ckv, kr4, pos0
  else:
    k_pos0 = pos0 - past_ckv.shape[1]
    ckv_all = jnp.concatenate([past_ckv, ckv], axis=1)
    kr4_all = jnp.concatenate([_quad(past_krope), kr4], axis=1)
  n_keys = ckv_all.shape[1]
  n_pad = -(-n_keys // LANES) * LANES
  k, v_t = _kv_up(_pad_axis(ckv_all, 1, n_pad), _pad_axis(kr4_all, 1, n_pad), p["wk"], p["wv_t"],
                  p["gkh"], _rope_table(k_pos0, n_pad), _tile(n_pad, 512, LANES))
  attend = functools.partial(
      _attn, n_valid=n_keys, q_pos0=pos0, k_pos0=k_pos0, hp=ATTN_HEADS_PER_STEP,
      tq=_tile(t_pad, ATTN_TQ, LANES), tk=_tile(n_pad, ATTN_TK, LANES))
  att = lax.cond(p["shift"][0] <= MAX_STATIC_SHIFT,
                 functools.partial(attend, online=False), functools.partial(attend, online=True),
                 q_t, k, v_t)[:, :t]

  y = _out(x, att, hs, ga, gb, p["wpa"], p["wpr"], p["wo"], p["g2"], p["wg"], p["wu"], p["wd"], tm)
  return y, ckv, kr, xtail[:, SUBLANES - (CONV_W - 1):, :], hl.reshape(b, d_rnn)


def kernel(x_prompt, x_sample, cache_ckv, cache_krope, state_conv, state_h,
           norm1_g, w_in, q_norm_g, w_uq, kv_norm_g, w_ukv, qk_q_g, qk_k_g,
           conv_w, conv_b, w_rg_a, b_rg_a, w_rg_x, b_rg_x, lru_lambda,
           w_proj_attn, w_proj_rnn, w_out, norm2_g, w_ffn_gate, w_ffn_up, w_ffn_down):
  depth = w_in.shape[0]
  past_len = cache_ckv.shape[2]
  y_p, y_s = x_prompt, x_sample
  outs_p, outs_s = [], []
  for l in range(depth):
    p = _prep_layer(l, norm1_g, w_in, q_norm_g, w_uq, kv_norm_g, w_ukv, qk_q_g, qk_k_g,
                    conv_w, conv_b, w_rg_a, b_rg_a, w_rg_x, b_rg_x, lru_lambda,
                    w_proj_attn, w_proj_rnn, w_out, norm2_g, w_ffn_gate, w_ffn_up, w_ffn_down)
    y_p, *new_p = _layer(y_p, 0, None, p)
    past = (cache_ckv[l], cache_krope[l], state_conv[l], state_h[l])
    y_s, *new_s = _layer(y_s, past_len, past, p)
    outs_p.append(new_p)
    outs_s.append(new_s)
  stack = lambda outs, i: jnp.stack([o[i] for o in outs])
  return (y_p, y_s,
          stack(outs_p, 0), stack(outs_p, 1), stack(outs_p, 2), stack(outs_p, 3),
          stack(outs_s, 0), stack(outs_s, 1), stack(outs_s, 2), stack(outs_s, 3))
```

```python
import functools
import math

import jax
import jax.numpy as jnp
from jax import lax
from jax.experimental import pallas as pl
from jax.experimental.pallas import tpu as pltpu

CHUNK = 64
N_HEADS = 8
QK_NOPE = 128
QK_ROPE = 64
QK_HEAD = QK_NOPE + QK_ROPE
V_HEAD = 128
ROPE_THETA = 10000.0
CONV_W = 4
LRU_C = 8.0
RNN_BLOCKS = 8
EPS = 1e-6
NEG_INF = -1e30
SCALE = QK_HEAD ** -0.5
LOG2E = 1.4426950408889634
F32_MIN_NORMAL = 1.1754943508222875e-38

LANES = 128
SUBLANES = 8
BF16_ROWS = 16
VMEM_LIMIT_BYTES = 58 * 1024 * 1024

HEAD_PAD = 2 * LANES
HALF = QK_ROPE // 2
CHUNK_SHIFT = CHUNK.bit_length() - 1
assert 1 << CHUNK_SHIFT == CHUNK

ROW_TILE = 512
UP_TILE = 1024
ATTN_HEADS_PER_STEP = 4
ATTN_TQ = 512
ATTN_TK = 512
ATTN_Q_SUBTILES = 2
SCORE_LEAD = 1
MAX_STATIC_SHIFT = 48.0
SHIFT_MARGIN = 1.02

F32 = jnp.float32
BF16 = jnp.bfloat16


def _tile(n, target, unit=BF16_ROWS):
  if n <= target:
    return n
  for t in range(target - target % unit, 0, -unit):
    if n % t == 0:
      return t
  return n


def _pad_axis(a, axis, n):
  if a.shape[axis] == n:
    return a
  widths = [(0, 0)] * a.ndim
  widths[axis] = (0, n - a.shape[axis])
  return jnp.pad(a, widths)


def _const_spec(shape):
  nd = len(shape)
  return pl.BlockSpec(shape, lambda *_: (0,) * nd, pipeline_mode=pl.Buffered(1))


def _params(*sem):
  return pltpu.CompilerParams(dimension_semantics=sem, vmem_limit_bytes=VMEM_LIMIT_BYTES)


def _rms(x, g):
  return x * lax.rsqrt(jnp.mean(x * x, axis=-1, keepdims=True) + EPS) * g


def _sigmoid(x):
  return 0.5 * jnp.tanh(0.5 * x) + 0.5


def _sqrt_nonneg(x):
  return x * lax.rsqrt(jnp.maximum(x, F32_MIN_NORMAL))


def _mm(a, b):
  return jnp.dot(a, b, preferred_element_type=F32)


HIST_ROWS = (CONV_W - 1) * SUBLANES
RNN_PARTS = 4


def _in_rnn_kernel(x_ref, g1_ref, w_ref, gq_ref, gkv_ref, hist_ref, h0_ref, cw_ref, cb_ref,
                   wgate_ref, ba_ref, bx_ref, lam_ref,
                   cqn_ref, ckv_ref, kr4_ref, kr_ref, xtail_ref, ga_ref, gb_ref, hs_ref, hl_ref,
                   xs, xcs, ras, ias, hcar, *, widths, tm):
  q_lora, kv_lora, d_rnn, d_model = widths
  o_kv, o_kr = q_lora, q_lora + kv_lora
  o_x = o_kr + LANES
  o_ga = o_x + d_rnn
  o_gb = o_ga + d_model
  rows = tm * SUBLANES
  blk = d_rnn // RNN_BLOCKS
  assert blk == LANES and RNN_BLOCKS == SUBLANES

  @pl.when(pl.program_id(1) == 0)
  def _():
    xs[0:HIST_ROWS, :] = hist_ref[0]
    hcar[...] = h0_ref[0]

  xn = _rms(x_ref[0], g1_ref[...]).astype(BF16)
  seg = lambda off, n: _mm(xn, w_ref[:, off:off + n])
  strided = lambda n: pl.ds(n, tm, stride=SUBLANES)
  tmajor = lambda v: v.reshape(tm, SUBLANES, LANES)

  xr = seg(o_x, d_rnn)
  xtail_ref[0] = xr[tm - SUBLANES:, :]
  for n in range(RNN_BLOCKS):
    xs[pl.ds(HIST_ROWS + n, tm, stride=SUBLANES), :] = xr[:, n * blk:(n + 1) * blk]
  xc = cb_ref[...][None] + tmajor(xs[0:rows, :]) * cw_ref[0][None]
  for j in range(1, CONV_W):
    xc = xc + tmajor(xs[j * SUBLANES:j * SUBLANES + rows, :]) * cw_ref[j][None]
  xcs[...] = xc.reshape(rows, LANES)

  z = -lam_ref[...]
  softplus = jnp.maximum(z, 0.0) + jnp.log1p(jnp.exp(-jnp.abs(z)))
  half_rate = (-0.5 * LRU_C) * softplus

  n_parts = RNN_PARTS if tm % (RNN_PARTS * SUBLANES) == 0 else 1
  pt = tm // n_parts

  def rnn_part(p, h):
    span = slice(p * pt * SUBLANES, (p + 1) * pt * SUBLANES)
    part = lambda v: v.reshape(pt, SUBLANES, LANES)
    for n in range(RNN_BLOCKS):
      rows_n = pl.ds(p * pt * SUBLANES + n, pt, stride=SUBLANES)
      gates = _mm(xcs[rows_n, :].astype(BF16), wgate_ref[n])
      ras[rows_n, :] = gates[:, :blk]
      ias[rows_n, :] = gates[:, blk:]
    r = _sigmoid(part(ras[span, :]) + ba_ref[...][None])
    i = _sigmoid(part(ias[span, :]) + bx_ref[...][None])
    th = jnp.tanh(half_rate[None] * r)
    rden = 1.0 / (1.0 - th)
    a = (1.0 + th) * rden
    ras[span, :] = a.reshape(pt * SUBLANES, LANES)
    u = _sqrt_nonneg((th * (-2.0 * rden)) * (1.0 + a)) * (i * part(xcs[span, :]))
    ias[span, :] = u.reshape(pt * SUBLANES, LANES)
    for t in range(p * pt, (p + 1) * pt):
      step = slice(t * SUBLANES, (t + 1) * SUBLANES)
      h = ras[step, :] * h + ias[step, :]
      ias[step, :] = h
    return h

  def q_cols():
    cqn_ref[0] = _rms(seg(0, q_lora), gq_ref[...]).astype(BF16)

  def kv_cols():
    ckv_ref[0] = _rms(seg(o_kv, kv_lora), gkv_ref[...])
    kr4 = seg(o_kr, LANES)
    kr4_ref[0] = kr4
    kr_ref[0] = kr4[:, :QK_ROPE]

  def ga_cols():
    ga_ref[0] = _sigmoid(seg(o_ga, d_model)).astype(BF16)

  def gb_cols():
    gb_ref[0] = _sigmoid(seg(o_gb, d_model)).astype(BF16)

  column_groups = [q_cols, kv_cols, ga_cols, gb_cols]
  h = hcar[...]
  for p in range(n_parts):
    h = rnn_part(p, h)
    for cols in column_groups[p::n_parts]:
      cols()
  hcar[...] = h
  hl_ref[0] = h
  for n in range(RNN_BLOCKS):
    hs_ref[0, :, n * blk:(n + 1) * blk] = ias[strided(n), :].astype(BF16)
  xs[0:HIST_ROWS, :] = xs[rows:rows + HIST_ROWS, :]


def _in_rnn(x, hist, h0, p, tm):
  b, t, d = x.shape
  widths = p["widths"]
  q_lora, kv_lora, d_rnn, d_model = widths
  row = lambda n: pl.BlockSpec((1, tm, n), lambda i, j: (i, j, 0))
  per_batch = lambda r: pl.BlockSpec((1, r, LANES), lambda i, j: (i, 0, 0))
  consts = [p["g1"], p["w_in"], p["gq"], p["gkv"]]
  rnn_consts = [p["cw"], p["cb"], p["w_gate"], p["ba"], p["bx"], p["lam"]]
  out_shape = (
      jax.ShapeDtypeStruct((b, t, q_lora), BF16),
      jax.ShapeDtypeStruct((b, t, kv_lora), F32),
      jax.ShapeDtypeStruct((b, t, LANES), F32),
      jax.ShapeDtypeStruct((b, t, QK_ROPE), F32),
      jax.ShapeDtypeStruct((b, SUBLANES, d_rnn), F32),
      jax.ShapeDtypeStruct((b, t, d_model), BF16),
      jax.ShapeDtypeStruct((b, t, d_model), BF16),
      jax.ShapeDtypeStruct((b, t, d_rnn), BF16),
      jax.ShapeDtypeStruct((b, SUBLANES, LANES), F32),
  )
  out_specs = (row(q_lora), row(kv_lora), row(LANES), row(QK_ROPE),
               pl.BlockSpec((1, SUBLANES, d_rnn), lambda i, j: (i, 0, 0)),
               row(d_model), row(d_model), row(d_rnn), per_batch(SUBLANES))
  rows = tm * SUBLANES
  return pl.pallas_call(
      functools.partial(_in_rnn_kernel, widths=widths, tm=tm),
      grid=(b, t // tm),
      in_specs=[row(d)] + [_const_spec(a.shape) for a in consts]
      + [per_batch(HIST_ROWS), per_batch(SUBLANES)] + [_const_spec(a.shape) for a in rnn_consts],
      out_specs=out_specs,
      out_shape=out_shape,
      scratch_shapes=[pltpu.VMEM((rows + HIST_ROWS, LANES), F32), pltpu.VMEM((rows, LANES), F32),
                      pltpu.VMEM((rows, LANES), F32), pltpu.VMEM((rows, LANES), F32),
                      pltpu.VMEM((SUBLANES, LANES), F32)],
      compiler_params=_params("parallel", "arbitrary"),
      name="in_rnn",
  )(x, *consts, hist, h0, *rnn_consts)


_NT = (((1,), (1,)), ((), ()))
SHIFT_COL = QK_HEAD


def _q_up_kernel(shift_ref, c_ref, w_ref, g_ref, cs_ref, q_ref, *, heads_per_dot):
  c = c_ref[0]
  tm = c.shape[0]
  g = jnp.concatenate([g_ref[...]] * (tm // LANES), axis=1)
  gn = g[:LANES]
  cos_g = cs_ref[:QK_ROPE, :] * g[LANES:LANES + QK_ROPE]
  sin_g = cs_ref[QK_ROPE:, :] * g[LANES + QK_ROPE:]
  rows = heads_per_dot * QK_HEAD
  pad_row = lax.broadcasted_iota(jnp.int32, (HEAD_PAD - SHIFT_COL, tm), 0)
  pad = jnp.where(pad_row == 0, -shift_ref[0], 0.0).astype(BF16)
  for h0 in range(0, N_HEADS, heads_per_dot):
    y_all = lax.dot_general(w_ref[h0 * QK_HEAD:h0 * QK_HEAD + rows, :], c, _NT,
                            preferred_element_type=F32)
    for d in range(heads_per_dot):
      h = h0 + d
      yn = y_all[d * QK_HEAD:d * QK_HEAD + QK_NOPE]
      x12 = y_all[d * QK_HEAD + QK_NOPE:(d + 1) * QK_HEAD]
      rot = jnp.concatenate([-x12[HALF:], x12[:HALF]], axis=0)
      ss = jnp.sum(yn * yn, axis=0, keepdims=True) + jnp.sum(x12 * x12, axis=0, keepdims=True)
      inv = lax.rsqrt(ss * (1.0 / QK_HEAD) + EPS) * (SCALE * LOG2E)
      q_ref[0, h, :LANES, :] = (yn * gn * inv).astype(BF16)
      q_ref[0, h, LANES:LANES + QK_ROPE, :] = ((x12 * cos_g + rot * sin_g) * inv).astype(BF16)
      q_ref[0, h, SHIFT_COL:, :] = pad


def _q_up(shift, cqn, w_t, g_t, cs_t, tm):
  b, t, r = cqn.shape
  return pl.pallas_call(
      functools.partial(_q_up_kernel, heads_per_dot=4),
      grid=(b, t // tm),
      in_specs=[pl.BlockSpec(memory_space=pltpu.SMEM),
                pl.BlockSpec((1, tm, r), lambda i, j: (i, j, 0)),
                _const_spec(w_t.shape), _const_spec(g_t.shape),
                pl.BlockSpec((LANES, tm), lambda i, j: (0, j))],
      out_specs=pl.BlockSpec((1, N_HEADS, HEAD_PAD, tm), lambda i, j: (i, 0, 0, j)),
      out_shape=jax.ShapeDtypeStruct((b, N_HEADS, HEAD_PAD, t), BF16),
      compiler_params=_params("parallel", "parallel"),
      name="q_up",
  )(shift, cqn, w_t, g_t, cs_t)


def _rope4(t4, cs, g4):
  t = t4 * cs * g4
  r = t + pltpu.roll(t, QK_ROPE, axis=1)
  lane = lax.broadcasted_iota(jnp.int32, r.shape, 1)
  return jnp.where(lane < QK_ROPE, r, 0.0)


def _kv_up_kernel(c_ref, kr4_ref, wk_ref, wv_ref, g_ref, cs_ref, k_ref, vt_ref):
  c = c_ref[0].astype(BF16)
  kr4 = kr4_ref[0]
  g = g_ref[...]
  kr = _rope4(kr4, cs_ref[...], g[:, LANES:])
  ssr = 0.5 * jnp.sum(kr4 * kr4, axis=-1, keepdims=True)
  lane = lax.broadcasted_iota(jnp.int32, (1, LANES), 1)
  one_hot = jnp.where(lane == SHIFT_COL - LANES, 1.0, 0.0)
  for h0 in range(0, N_HEADS, 2):
    y = _mm(c, wk_ref[:, h0 * QK_NOPE:(h0 + 2) * QK_NOPE])
    for d in range(2):
      kn = y[:, d * QK_NOPE:(d + 1) * QK_NOPE]
      ss = jnp.sum(kn * kn, axis=-1, keepdims=True) + ssr
      inv = lax.rsqrt(ss * (1.0 / QK_HEAD) + EPS)
      k_ref[0, h0 + d, :, :LANES] = (kn * inv * g[:, :LANES]).astype(BF16)
      k_ref[0, h0 + d, :, LANES:] = (kr * inv + one_hot).astype(BF16)
  vt = lax.dot_general(wv_ref[...], c, _NT, preferred_element_type=F32)
  for h in range(N_HEADS):
    vt_ref[0, h] = vt[h * V_HEAD:(h + 1) * V_HEAD].astype(BF16)


def _kv_up(ckv, kr4, wk, wv_t, g, cs, tm):
  b, t, r = ckv.shape
  return pl.pallas_call(
      _kv_up_kernel,
      grid=(b, t // tm),
      in_specs=[pl.BlockSpec((1, tm, r), lambda i, j: (i, j, 0)),
                pl.BlockSpec((1, tm, LANES), lambda i, j: (i, j, 0)),
                _const_spec(wk.shape), _const_spec(wv_t.shape), _const_spec(g.shape),
                pl.BlockSpec((tm, LANES), lambda i, j: (j, 0))],
      out_specs=(pl.BlockSpec((1, N_HEADS, tm, HEAD_PAD), lambda i, j: (i, 0, j, 0)),
                 pl.BlockSpec((1, N_HEADS, V_HEAD, tm), lambda i, j: (i, 0, 0, j))),
      out_shape=(jax.ShapeDtypeStruct((b, N_HEADS, t, HEAD_PAD), BF16),
                 jax.ShapeDtypeStruct((b, N_HEADS, V_HEAD, t), BF16)),
      compiler_params=_params("parallel", "parallel"),
      name="kv_up",
  )(ckv, kr4, wk, wv_t, g, cs)


def _attn_kernel(q_ref, k_ref, vt_ref, o_ref, acc_sc, ml_sc, s0_sc, s1_sc,
                 *, hp, tq, qs, tk, n_valid, q_pos0, k_pos0, online):
  def score(a, j, s_sc, h):
    ks = pl.multiple_of(j * tk, tk)
    s_sc[h] = _mm(k_ref[0, h, pl.ds(ks, tk), :], q_ref[0, h, :, a * tq:(a + 1) * tq])

  def tile(a, j, q_chunk, s_cur, s_next, masked, nxt):
    lead = min(SCORE_LEAD, hp)
    if nxt is not None:
      for h in range(lead):
        score(*nxt, s_next, h)
    ks = pl.multiple_of(j * tk, tk)
    if masked:
      k_idx = ks + lax.broadcasted_iota(jnp.int32, (tk, 1), 0)
      k_chunk = jnp.where(k_idx < n_valid, (k_pos0 + k_idx) >> CHUNK_SHIFT, jnp.iinfo(jnp.int32).max)
      visible = k_chunk <= q_chunk
    for h in range(hp):
      m, l = ml_sc[2 * h], ml_sc[2 * h + 1]
      st = s_cur[h]
      if masked:
        st = jnp.where(visible, st, NEG_INF)
      if online:
        m_new = jnp.maximum(m, jnp.max(st, axis=0, keepdims=True))
        alpha = jnp.exp2(m - m_new)
        p = jnp.exp2(st - m_new)
        ml_sc[2 * h] = m_new
        ml_sc[2 * h + 1] = alpha * l + jnp.sum(p, axis=0, keepdims=True)
      else:
        p = jnp.exp2(st)
        ml_sc[2 * h + 1] = l + jnp.sum(p, axis=0, keepdims=True)
      pv = _mm(vt_ref[0, h, :, pl.ds(ks, tk)], p.astype(BF16))
      acc_sc[h] = alpha * acc_sc[h] + pv if online else acc_sc[h] + pv
      if nxt is not None and h + lead < hp:
        score(*nxt, s_next, h + lead)

  def step(a, j, done, q_chunk, masked, nxt):
    parity = (done + j) & 1

    @pl.when(parity == 0)
    def _():
      tile(a, j, q_chunk, s0_sc, s1_sc, masked, nxt)

    @pl.when(parity == 1)
    def _():
      tile(a, j, q_chunk, s1_sc, s0_sc, masked, nxt)

  for h in range(hp):
    score(0, 0, s0_sc, h)
  done = 0
  for a in range(qs):
    row0 = q_pos0 + (pl.program_id(2) * qs + a) * tq
    keys_all = jnp.clip(((row0 >> CHUNK_SHIFT) + 1) * CHUNK - k_pos0, 0, n_valid)
    keys_any = jnp.clip((((row0 + tq - 1) >> CHUNK_SHIFT) + 1) * CHUNK - k_pos0, 0, n_valid)
    last = (keys_any + tk - 1) // tk - 1
    n_open = jnp.minimum(keys_all // tk, last)
    q_chunk = (row0 + lax.broadcasted_iota(jnp.int32, (1, tq), 1)) >> CHUNK_SHIFT
    acc_sc[...] = jnp.zeros(acc_sc.shape, F32)
    for h in range(hp):
      ml_sc[2 * h] = jnp.full((1, tq), NEG_INF, F32)
      ml_sc[2 * h + 1] = jnp.zeros((1, tq), F32)

    def loop_body(j, carry, masked, a=a, done=done, q_chunk=q_chunk):
      step(a, j, done, q_chunk, masked, (a, j + 1))
      return carry

    lax.fori_loop(0, n_open, functools.partial(loop_body, masked=False), 0)
    lax.fori_loop(n_open, last, functools.partial(loop_body, masked=True), 0)
    step(a, last, done, q_chunk, True, (a + 1, 0) if a + 1 < qs else None)
    for h in range(hp):
      o = acc_sc[h] * (1.0 / ml_sc[2 * h + 1])
      o_ref[0, a * tq:(a + 1) * tq, h * V_HEAD:(h + 1) * V_HEAD] = o.T.astype(BF16)
    done = done + last + 1


def _attn(q_t, k, v_t, n_valid, q_pos0, k_pos0, hp, tq, tk, online):
  b, h, _, t = q_t.shape
  n_keys = k.shape[2]
  assert q_pos0 >= k_pos0 and n_keys % tk == 0 and t % tq == 0 and h % hp == 0
  qs = ATTN_Q_SUBTILES if (t // tq) % ATTN_Q_SUBTILES == 0 else 1
  return pl.pallas_call(
      functools.partial(_attn_kernel, hp=hp, tq=tq, qs=qs, tk=tk, n_valid=n_valid,
                        q_pos0=q_pos0, k_pos0=k_pos0, online=online),
      grid=(b, h // hp, t // (qs * tq)),
      in_specs=[pl.BlockSpec((1, hp, HEAD_PAD, qs * tq), lambda i, j, l: (i, j, 0, l)),
                pl.BlockSpec((1, hp, n_keys, HEAD_PAD), lambda i, j, l: (i, j, 0, 0)),
                pl.BlockSpec((1, hp, V_HEAD, n_keys), lambda i, j, l: (i, j, 0, 0))],
      out_specs=pl.BlockSpec((1, qs * tq, hp * V_HEAD), lambda i, j, l: (i, l, j)),
      out_shape=jax.ShapeDtypeStruct((b, t, h * V_HEAD), BF16),
      scratch_shapes=[pltpu.VMEM((hp, V_HEAD, tq), F32), pltpu.VMEM((2 * hp, 1, tq), F32),
                      pltpu.VMEM((hp, tk, tq), F32), pltpu.VMEM((hp, tk, tq), F32)],
      compiler_params=_params("parallel", "parallel", "arbitrary"),
      name="attn",
  )(q_t, k, v_t)


def _out_kernel(x_ref, att_ref, hs_ref, ga_ref, gb_ref, wpa_ref, wpr_ref, wo_ref, g2_ref,
                wg_ref, wu_ref, wd_ref, y_ref, *, ff_chunk):
  o_att = _mm(att_ref[0], wpa_ref[...])
  o_rnn = _mm(hs_ref[0], wpr_ref[...])
  merged = ga_ref[0].astype(F32) * o_att + gb_ref[0].astype(F32) * o_rnn
  x1 = x_ref[0] + _mm(merged.astype(BF16), wo_ref[...])
  xn = _rms(x1, g2_ref[...]).astype(BF16)
  acc = x1
  for c0 in range(0, wg_ref.shape[1], ff_chunk):
    g = _mm(xn, wg_ref[:, c0:c0 + ff_chunk])
    u = _mm(xn, wu_ref[:, c0:c0 + ff_chunk])
    acc = acc + _mm((g * _sigmoid(g) * u).astype(BF16), wd_ref[c0:c0 + ff_chunk, :])
  y_ref[0] = acc


def _out(x, att, hs, ga, gb, wpa, wpr, wo, g2, wg, wu, wd, tm):
  b, t, d = x.shape
  d_ff = wg.shape[1]
  ff_chunk = 2 * LANES if d_ff % (2 * LANES) == 0 else d_ff
  row = lambda n: pl.BlockSpec((1, tm, n), lambda i, j: (i, j, 0))
  consts = [wpa, wpr, wo, g2, wg, wu, wd]
  return pl.pallas_call(
      functools.partial(_out_kernel, ff_chunk=ff_chunk),
      grid=(b, t // tm),
      in_specs=[row(d), row(att.shape[2]), row(hs.shape[2]), row(d), row(d)]
      + [_const_spec(a.shape) for a in consts],
      out_specs=row(d),
      out_shape=jax.ShapeDtypeStruct((b, t, d), F32),
      compiler_params=_params("parallel", "parallel"),
      name="out",
  )(x, att, hs, ga, gb, *consts)


def _quad(a):
  x1, x2 = a[..., :HALF], a[..., HALF:]
  return jnp.concatenate([x1, x2, -x2, x1], axis=-1)


def _gain_row(g):
  g1, g2 = g[QK_NOPE:QK_NOPE + HALF], g[QK_NOPE + HALF:]
  return jnp.concatenate([g[:QK_NOPE], g1, g2, g2, g1])[None, :]


def _rope_table(pos0, n):
  inv_freq = jnp.exp(-math.log(ROPE_THETA) * jnp.arange(HALF, dtype=F32) / HALF)
  ang = (pos0 + jnp.arange(n, dtype=jnp.int32)).astype(F32)[:, None] * inv_freq[None, :]
  c, s = jnp.cos(ang), jnp.sin(ang)
  return jnp.concatenate([c, c, s, s], axis=-1)


def _prep_layer(l, norm1_g, w_in, q_norm_g, w_uq, kv_norm_g, w_ukv, qk_q_g, qk_k_g,
                conv_w, conv_b, w_rg_a, b_rg_a, w_rg_x, b_rg_x, lru_lambda,
                w_proj_attn, w_proj_rnn, w_out, norm2_g, w_ffn_gate, w_ffn_up, w_ffn_down):
  q_lora, kv_lora = q_norm_g.shape[1], kv_norm_g.shape[1]
  d_rnn, d_model = conv_b.shape[1], norm1_g.shape[1]
  o_kr = q_lora + kv_lora
  o_x = o_kr + QK_ROPE
  w = w_in[l]
  w_in4 = jnp.concatenate([w[:, :o_kr], _quad(w[:, o_kr:o_x]), w[:, o_x:]], axis=1).astype(BF16)
  wq_t = w_uq[l].T.astype(BF16)
  wkv = w_ukv[l].reshape(kv_lora, N_HEADS, QK_NOPE + V_HEAD)
  wk = wkv[..., :QK_NOPE].reshape(kv_lora, N_HEADS * QK_NOPE).astype(BF16)
  wv_t = wkv[..., QK_NOPE:].reshape(kv_lora, N_HEADS * V_HEAD).T.astype(BF16)
  w_gate = jnp.concatenate([w_rg_a[l], w_rg_x[l]], axis=-1).astype(BF16)
  shift = (SHIFT_MARGIN * QK_HEAD * SCALE * LOG2E) * (
      jnp.max(jnp.abs(qk_q_g[l])) * jnp.max(jnp.abs(qk_k_g[l])))
  row = lambda a: a[l][None, :]
  tile8 = lambda a: a[l].reshape(SUBLANES, LANES)
  return dict(
      widths=(q_lora, kv_lora, d_rnn, d_model),
      g1=row(norm1_g), w_in=w_in4, gq=row(q_norm_g), gkv=row(kv_norm_g),
      wq_t=wq_t, gq_t=jnp.broadcast_to(_gain_row(qk_q_g[l]).T, (HEAD_PAD, LANES)),
      wk=wk, wv_t=wv_t, gkh=_gain_row(qk_k_g[l]), shift=shift.reshape(1).astype(F32),
      cw=conv_w[l].reshape(CONV_W, SUBLANES, LANES), cb=tile8(conv_b), w_gate=w_gate,
      ba=tile8(b_rg_a), bx=tile8(b_rg_x), lam=tile8(lru_lambda),
      wpa=w_proj_attn[l].astype(BF16), wpr=w_proj_rnn[l].astype(BF16), wo=w_out[l].astype(BF16),
      g2=row(norm2_g), wg=w_ffn_gate[l].astype(BF16), wu=w_ffn_up[l].astype(BF16),
      wd=w_ffn_down[l].astype(BF16))


def _layer(x, pos0, past, p):
  b, t, _ = x.shape
  d_rnn = p["widths"][2]
  assert t >= SUBLANES and t % SUBLANES == 0 and d_rnn == SUBLANES * LANES
  if past is None:
    hist = jnp.zeros((b, HIST_ROWS, LANES), F32)
    h0 = jnp.zeros((b, SUBLANES, LANES), F32)
  else:
    past_ckv, past_krope, conv_hist, state_h = past
    hist = conv_hist.reshape(b, HIST_ROWS, LANES)
    h0 = state_h.reshape(b, SUBLANES, LANES)
  tm = _tile(t, ROW_TILE)
  cqn, ckv, kr4, kr, xtail, ga, gb, hs, hl = _in_rnn(x, hist, h0, p, tm)

  t_pad = -(-t // LANES) * LANES
  q_t = _q_up(p["shift"], _pad_axis(cqn, 1, t_pad), p["wq_t"], p["gq_t"],
              _rope_table(pos0, t_pad).T, _tile(t_pad, UP_TILE, LANES))
  if past is None:
    ckv_all, kr4_all, k_pos0 = ckv, kr4, pos0
  else:
    k_pos0 = pos0 - past_ckv.shape[1]
    ckv_all = jnp.concatenate([past_ckv, ckv], axis=1)
    kr4_all = jnp.concatenate([_quad(past_krope), kr4], axis=1)
  n_keys = ckv_all.shape[1]
  n_pad = -(-n_keys // LANES) * LANES
  k, v_t = _kv_up(_pad_axis(ckv_all, 1, n_pad), _pad_axis(kr4_all, 1, n_pad), p["wk"], p["wv_t"],
                  p["gkh"], _rope_table(k_pos0, n_pad), _tile(n_pad, UP_TILE, LANES))
  attend = functools.partial(
      _attn, n_valid=n_keys, q_pos0=pos0, k_pos0=k_pos0, hp=ATTN_HEADS_PER_STEP,
      tq=_tile(t_pad, ATTN_TQ, LANES), tk=_tile(n_pad, ATTN_TK, LANES))
  att = lax.cond(p["shift"][0] <= MAX_STATIC_SHIFT,
                 functools.partial(attend, online=False), functools.partial(attend, online=True),
                 q_t, k, v_t)[:, :t]

  row_inputs = (x, att, hs, ga, gb)
  if t < ROW_TILE:
    row_inputs = tuple(a.reshape(1, b * t, a.shape[2]) for a in row_inputs)
  y = _out(*row_inputs, p["wpa"], p["wpr"], p["wo"], p["g2"], p["wg"], p["wu"], p["wd"],
           _tile(row_inputs[0].shape[1], ROW_TILE)).reshape(x.shape)
  return y, ckv, kr, xtail[:, SUBLANES - (CONV_W - 1):, :], hl.reshape(b, d_rnn)


def kernel(x_prompt, x_sample, cache_ckv, cache_krope, state_conv, state_h,
           norm1_g, w_in, q_norm_g, w_uq, kv_norm_g, w_ukv, qk_q_g, qk_k_g,
           conv_w, conv_b, w_rg_a, b_rg_a, w_rg_x, b_rg_x, lru_lambda,
           w_proj_attn, w_proj_rnn, w_out, norm2_g, w_ffn_gate, w_ffn_up, w_ffn_down):
  depth = w_in.shape[0]
  past_len = cache_ckv.shape[2]
  y_p, y_s = x_prompt, x_sample
  outs_p, outs_s = [], []
  for l in range(depth):
    p = _prep_layer(l, norm1_g, w_in, q_norm_g, w_uq, kv_norm_g, w_ukv, qk_q_g, qk_k_g,
                    conv_w, conv_b, w_rg_a, b_rg_a, w_rg_x, b_rg_x, lru_lambda,
                    w_proj_attn, w_proj_rnn, w_out, norm2_g, w_ffn_gate, w_ffn_up, w_ffn_down)
    y_p, *new_p = _layer(y_p, 0, None, p)
    past = (cache_ckv[l], cache_krope[l], state_conv[l], state_h[l])
    y_s, *new_s = _layer(y_s, past_len, past, p)
    outs_p.append(new_p)
    outs_s.append(new_s)
  stack = lambda outs, i: jnp.stack([o[i] for o in outs])
  return (y_p, y_s,
          stack(outs_p, 0), stack(outs_p, 1), stack(outs_p, 2), stack(outs_p, 3),
          stack(outs_s, 0), stack(outs_s, 1), stack(outs_s, 2), stack(outs_s, 3))
```

```python
import functools
import math

import jax
import jax.numpy as jnp
from jax import lax
from jax.experimental import pallas as pl
from jax.experimental.pallas import tpu as pltpu

CHUNK = 64
N_HEADS = 8
QK_NOPE = 128
QK_ROPE = 64
QK_HEAD = QK_NOPE + QK_ROPE
V_HEAD = 128
ROPE_THETA = 10000.0
CONV_W = 4
LRU_C = 8.0
RNN_BLOCKS = 8
EPS = 1e-6
NEG_INF = -1e30
SCALE = QK_HEAD ** -0.5
LOG2E = 1.4426950408889634
F32_MIN_NORMAL = 1.1754943508222875e-38

LANES = 128
SUBLANES = 8
BF16_ROWS = 16
VMEM_LIMIT_BYTES = 58 * 1024 * 1024

HEAD_PAD = 2 * LANES
HALF = QK_ROPE // 2
CHUNK_SHIFT = CHUNK.bit_length() - 1
assert 1 << CHUNK_SHIFT == CHUNK

ROW_TILE = 512
UP_TILE = 1024
ATTN_HEADS_PER_STEP = 4
ATTN_TQ = 512
ATTN_TK = 512
ATTN_Q_SUBTILES = 4
SCORE_LEAD = 1
MAX_STATIC_SHIFT = 48.0
SHIFT_MARGIN = 1.02

F32 = jnp.float32
BF16 = jnp.bfloat16


def _tile(n, target, unit=BF16_ROWS):
  if n <= target:
    return n
  for t in range(target - target % unit, 0, -unit):
    if n % t == 0:
      return t
  return n


def _pad_axis(a, axis, n):
  if a.shape[axis] == n:
    return a
  widths = [(0, 0)] * a.ndim
  widths[axis] = (0, n - a.shape[axis])
  return jnp.pad(a, widths)


def _const_spec(shape):
  nd = len(shape)
  return pl.BlockSpec(shape, lambda *_: (0,) * nd, pipeline_mode=pl.Buffered(1))


def _params(*sem):
  return pltpu.CompilerParams(dimension_semantics=sem, vmem_limit_bytes=VMEM_LIMIT_BYTES)


def _rms(x, g):
  return x * lax.rsqrt(jnp.mean(x * x, axis=-1, keepdims=True) + EPS) * g


def _sigmoid(x):
  return 0.5 * jnp.tanh(0.5 * x) + 0.5


def _sqrt_nonneg(x):
  return x * lax.rsqrt(jnp.maximum(x, F32_MIN_NORMAL))


def _mm(a, b):
  return jnp.dot(a, b, preferred_element_type=F32)


HIST_ROWS = (CONV_W - 1) * SUBLANES
RNN_PARTS = 4


def _in_rnn_kernel(x_ref, g1_ref, w_ref, gq_ref, gkv_ref, hist_ref, h0_ref, cw_ref, cb_ref,
                   wgate_ref, ba_ref, bx_ref, lam_ref,
                   cqn_ref, ckv_ref, kr4_ref, kr_ref, xtail_ref, ga_ref, gb_ref, hs_ref, hl_ref,
                   xs, xcs, ras, ias, hcar, *, widths, tm):
  q_lora, kv_lora, d_rnn, d_model = widths
  o_kv, o_kr = q_lora, q_lora + kv_lora
  o_x = o_kr + LANES
  o_ga = o_x + d_rnn
  o_gb = o_ga + d_model
  rows = tm * SUBLANES
  blk = d_rnn // RNN_BLOCKS
  assert blk == LANES and RNN_BLOCKS == SUBLANES

  @pl.when(pl.program_id(1) == 0)
  def _():
    xs[0:HIST_ROWS, :] = hist_ref[0]
    hcar[...] = h0_ref[0]

  xn = _rms(x_ref[0], g1_ref[...]).astype(BF16)
  seg = lambda off, n: _mm(xn, w_ref[:, off:off + n])
  strided = lambda n: pl.ds(n, tm, stride=SUBLANES)
  tmajor = lambda v: v.reshape(tm, SUBLANES, LANES)

  xr = seg(o_x, d_rnn)
  xtail_ref[0] = xr[tm - SUBLANES:, :]
  for n in range(RNN_BLOCKS):
    xs[pl.ds(HIST_ROWS + n, tm, stride=SUBLANES), :] = xr[:, n * blk:(n + 1) * blk]
  xc = cb_ref[...][None] + tmajor(xs[0:rows, :]) * cw_ref[0][None]
  for j in range(1, CONV_W):
    xc = xc + tmajor(xs[j * SUBLANES:j * SUBLANES + rows, :]) * cw_ref[j][None]
  xcs[...] = xc.reshape(rows, LANES)

  z = -lam_ref[...]
  softplus = jnp.maximum(z, 0.0) + jnp.log1p(jnp.exp(-jnp.abs(z)))
  half_rate = (-0.5 * LRU_C) * softplus

  n_parts = RNN_PARTS if tm % (RNN_PARTS * SUBLANES) == 0 else 1
  pt = tm // n_parts

  def rnn_part(p, h):
    span = slice(p * pt * SUBLANES, (p + 1) * pt * SUBLANES)
    part = lambda v: v.reshape(pt, SUBLANES, LANES)
    for n in range(RNN_BLOCKS):
      rows_n = pl.ds(p * pt * SUBLANES + n, pt, stride=SUBLANES)
      gates = _mm(xcs[rows_n, :].astype(BF16), wgate_ref[n])
      ras[rows_n, :] = gates[:, :blk]
      ias[rows_n, :] = gates[:, blk:]
    r = _sigmoid(part(ras[span, :]) + ba_ref[...][None])
    i = _sigmoid(part(ias[span, :]) + bx_ref[...][None])
    th = jnp.tanh(half_rate[None] * r)
    rden = 1.0 / (1.0 - th)
    a = (1.0 + th) * rden
    ras[span, :] = a.reshape(pt * SUBLANES, LANES)
    u = _sqrt_nonneg((th * (-2.0 * rden)) * (1.0 + a)) * (i * part(xcs[span, :]))
    ias[span, :] = u.reshape(pt * SUBLANES, LANES)
    for t in range(p * pt, (p + 1) * pt):
      step = slice(t * SUBLANES, (t + 1) * SUBLANES)
      h = ras[step, :] * h + ias[step, :]
      ias[step, :] = h
    return h

  def q_cols():
    cqn_ref[0] = _rms(seg(0, q_lora), gq_ref[...]).astype(BF16)

  def kv_cols():
    ckv_ref[0] = _rms(seg(o_kv, kv_lora), gkv_ref[...])
    kr4 = seg(o_kr, LANES)
    kr4_ref[0] = kr4
    kr_ref[0] = kr4[:, :QK_ROPE]

  def ga_cols():
    ga_ref[0] = _sigmoid(seg(o_ga, d_model)).astype(BF16)

  def gb_cols():
    gb_ref[0] = _sigmoid(seg(o_gb, d_model)).astype(BF16)

  column_groups = [q_cols, kv_cols, ga_cols, gb_cols]
  h = hcar[...]
  for p in range(n_parts):
    h = rnn_part(p, h)
    for cols in column_groups[p::n_parts]:
      cols()
  hcar[...] = h
  hl_ref[0] = h
  for n in range(RNN_BLOCKS):
    hs_ref[0, :, n * blk:(n + 1) * blk] = ias[strided(n), :].astype(BF16)
  xs[0:HIST_ROWS, :] = xs[rows:rows + HIST_ROWS, :]


def _in_rnn(x, hist, h0, p, tm):
  b, t, d = x.shape
  widths = p["widths"]
  q_lora, kv_lora, d_rnn, d_model = widths
  row = lambda n: pl.BlockSpec((1, tm, n), lambda i, j: (i, j, 0))
  per_batch = lambda r: pl.BlockSpec((1, r, LANES), lambda i, j: (i, 0, 0))
  consts = [p["g1"], p["w_in"], p["gq"], p["gkv"]]
  rnn_consts = [p["cw"], p["cb"], p["w_gate"], p["ba"], p["bx"], p["lam"]]
  out_shape = (
      jax.ShapeDtypeStruct((b, t, q_lora), BF16),
      jax.ShapeDtypeStruct((b, t, kv_lora), F32),
      jax.ShapeDtypeStruct((b, t, LANES), F32),
      jax.ShapeDtypeStruct((b, t, QK_ROPE), F32),
      jax.ShapeDtypeStruct((b, SUBLANES, d_rnn), F32),
      jax.ShapeDtypeStruct((b, t, d_model), BF16),
      jax.ShapeDtypeStruct((b, t, d_model), BF16),
      jax.ShapeDtypeStruct((b, t, d_rnn), BF16),
      jax.ShapeDtypeStruct((b, SUBLANES, LANES), F32),
  )
  out_specs = (row(q_lora), row(kv_lora), row(LANES), row(QK_ROPE),
               pl.BlockSpec((1, SUBLANES, d_rnn), lambda i, j: (i, 0, 0)),
               row(d_model), row(d_model), row(d_rnn), per_batch(SUBLANES))
  rows = tm * SUBLANES
  return pl.pallas_call(
      functools.partial(_in_rnn_kernel, widths=widths, tm=tm),
      grid=(b, t // tm),
      in_specs=[row(d)] + [_const_spec(a.shape) for a in consts]
      + [per_batch(HIST_ROWS), per_batch(SUBLANES)] + [_const_spec(a.shape) for a in rnn_consts],
      out_specs=out_specs,
      out_shape=out_shape,
      scratch_shapes=[pltpu.VMEM((rows + HIST_ROWS, LANES), F32), pltpu.VMEM((rows, LANES), F32),
                      pltpu.VMEM((rows, LANES), F32), pltpu.VMEM((rows, LANES), F32),
                      pltpu.VMEM((SUBLANES, LANES), F32)],
      compiler_params=_params("parallel", "arbitrary"),
      name="in_rnn",
  )(x, *consts, hist, h0, *rnn_consts)


_NT = (((1,), (1,)), ((), ()))
SHIFT_COL = QK_HEAD


def _q_up_kernel(shift_ref, c_ref, w_ref, g_ref, cs_ref, q_ref, *, heads_per_dot):
  c = c_ref[0]
  tm = c.shape[0]
  g = jnp.concatenate([g_ref[...]] * (tm // LANES), axis=1)
  gn = g[:LANES]
  cos_g = cs_ref[:QK_ROPE, :] * g[LANES:LANES + QK_ROPE]
  sin_g = cs_ref[QK_ROPE:, :] * g[LANES + QK_ROPE:]
  rows = heads_per_dot * QK_HEAD
  pad_row = lax.broadcasted_iota(jnp.int32, (HEAD_PAD - SHIFT_COL, tm), 0)
  pad = jnp.where(pad_row == 0, -shift_ref[0], 0.0).astype(BF16)
  for h0 in range(0, N_HEADS, heads_per_dot):
    y_all = lax.dot_general(w_ref[h0 * QK_HEAD:h0 * QK_HEAD + rows, :], c, _NT,
                            preferred_element_type=F32)
    for d in range(heads_per_dot):
      h = h0 + d
      yn = y_all[d * QK_HEAD:d * QK_HEAD + QK_NOPE]
      x12 = y_all[d * QK_HEAD + QK_NOPE:(d + 1) * QK_HEAD]
      rot = jnp.concatenate([-x12[HALF:], x12[:HALF]], axis=0)
      ss = jnp.sum(yn * yn, axis=0, keepdims=True) + jnp.sum(x12 * x12, axis=0, keepdims=True)
      inv = lax.rsqrt(ss * (1.0 / QK_HEAD) + EPS) * (SCALE * LOG2E)
      q_ref[0, h, :LANES, :] = (yn * gn * inv).astype(BF16)
      q_ref[0, h, LANES:LANES + QK_ROPE, :] = ((x12 * cos_g + rot * sin_g) * inv).astype(BF16)
      q_ref[0, h, SHIFT_COL:, :] = pad


def _q_up(shift, cqn, w_t, g_t, cs_t, tm):
  b, t, r = cqn.shape
  return pl.pallas_call(
      functools.partial(_q_up_kernel, heads_per_dot=4),
      grid=(b, t // tm),
      in_specs=[pl.BlockSpec(memory_space=pltpu.SMEM),
                pl.BlockSpec((1, tm, r), lambda i, j: (i, j, 0)),
                _const_spec(w_t.shape), _const_spec(g_t.shape),
                pl.BlockSpec((LANES, tm), lambda i, j: (0, j))],
      out_specs=pl.BlockSpec((1, N_HEADS, HEAD_PAD, tm), lambda i, j: (i, 0, 0, j)),
      out_shape=jax.ShapeDtypeStruct((b, N_HEADS, HEAD_PAD, t), BF16),
      compiler_params=_params("parallel", "parallel"),
      name="q_up",
  )(shift, cqn, w_t, g_t, cs_t)


def _rope4(t4, cs, g4):
  t = t4 * cs * g4
  r = t + pltpu.roll(t, QK_ROPE, axis=1)
  lane = lax.broadcasted_iota(jnp.int32, r.shape, 1)
  return jnp.where(lane < QK_ROPE, r, 0.0)


def _kv_up_kernel(c_ref, kr4_ref, wk_ref, wv_ref, g_ref, cs_ref, k_ref, vt_ref):
  c = c_ref[0].astype(BF16)
  kr4 = kr4_ref[0]
  g = g_ref[...]
  kr = _rope4(kr4, cs_ref[...], g[:, LANES:])
  ssr = 0.5 * jnp.sum(kr4 * kr4, axis=-1, keepdims=True)
  lane = lax.broadcasted_iota(jnp.int32, (1, LANES), 1)
  one_hot = jnp.where(lane == SHIFT_COL - LANES, 1.0, 0.0)
  for h0 in range(0, N_HEADS, 2):
    y = _mm(c, wk_ref[:, h0 * QK_NOPE:(h0 + 2) * QK_NOPE])
    for d in range(2):
      kn = y[:, d * QK_NOPE:(d + 1) * QK_NOPE]
      ss = jnp.sum(kn * kn, axis=-1, keepdims=True) + ssr
      inv = lax.rsqrt(ss * (1.0 / QK_HEAD) + EPS)
      k_ref[0, h0 + d, :, :LANES] = (kn * inv * g[:, :LANES]).astype(BF16)
      k_ref[0, h0 + d, :, LANES:] = (kr * inv + one_hot).astype(BF16)
  vt = lax.dot_general(wv_ref[...], c, _NT, preferred_element_type=F32)
  for h in range(N_HEADS):
    vt_ref[0, h] = vt[h * V_HEAD:(h + 1) * V_HEAD].astype(BF16)


def _kv_up(ckv, kr4, wk, wv_t, g, cs, tm):
  b, t, r = ckv.shape
  return pl.pallas_call(
      _kv_up_kernel,
      grid=(b, t // tm),
      in_specs=[pl.BlockSpec((1, tm, r), lambda i, j: (i, j, 0)),
                pl.BlockSpec((1, tm, LANES), lambda i, j: (i, j, 0)),
                _const_spec(wk.shape), _const_spec(wv_t.shape), _const_spec(g.shape),
                pl.BlockSpec((tm, LANES), lambda i, j: (j, 0))],
      out_specs=(pl.BlockSpec((1, N_HEADS, tm, HEAD_PAD), lambda i, j: (i, 0, j, 0)),
                 pl.BlockSpec((1, N_HEADS, V_HEAD, tm), lambda i, j: (i, 0, 0, j))),
      out_shape=(jax.ShapeDtypeStruct((b, N_HEADS, t, HEAD_PAD), BF16),
                 jax.ShapeDtypeStruct((b, N_HEADS, V_HEAD, t), BF16)),
      compiler_params=_params("parallel", "parallel"),
      name="kv_up",
  )(ckv, kr4, wk, wv_t, g, cs)


def _attn_kernel(q_ref, k_ref, vt_ref, o_ref, acc_sc, ml_sc, s0_sc, s1_sc,
                 *, hp, tq, qs, tk, n_valid, q_pos0, k_pos0, online):
  def score(a, j, s_sc, h):
    ks = pl.multiple_of(j * tk, tk)
    s_sc[h] = _mm(k_ref[0, h, pl.ds(ks, tk), :], q_ref[0, h, :, a * tq:(a + 1) * tq])

  def tile(a, j, q_chunk, s_cur, s_next, masked, nxt):
    lead = min(SCORE_LEAD, hp)
    if nxt is not None:
      for h in range(lead):
        score(*nxt, s_next, h)
    ks = pl.multiple_of(j * tk, tk)
    if masked:
      k_idx = ks + lax.broadcasted_iota(jnp.int32, (tk, 1), 0)
      k_chunk = jnp.where(k_idx < n_valid, (k_pos0 + k_idx) >> CHUNK_SHIFT, jnp.iinfo(jnp.int32).max)
      visible = k_chunk <= q_chunk
    for h in range(hp):
      m, l = ml_sc[2 * h], ml_sc[2 * h + 1]
      st = s_cur[h]
      if masked:
        st = jnp.where(visible, st, NEG_INF)
      if online:
        m_new = jnp.maximum(m, jnp.max(st, axis=0, keepdims=True))
        alpha = jnp.exp2(m - m_new)
        p = jnp.exp2(st - m_new)
        ml_sc[2 * h] = m_new
        ml_sc[2 * h + 1] = alpha * l + jnp.sum(p, axis=0, keepdims=True)
      else:
        p = jnp.exp2(st)
        ml_sc[2 * h + 1] = l + jnp.sum(p, axis=0, keepdims=True)
      pv = _mm(vt_ref[0, h, :, pl.ds(ks, tk)], p.astype(BF16))
      acc_sc[h] = alpha * acc_sc[h] + pv if online else acc_sc[h] + pv
      if nxt is not None and h + lead < hp:
        score(*nxt, s_next, h + lead)

  def step(a, j, done, q_chunk, masked, nxt):
    parity = (done + j) & 1

    @pl.when(parity == 0)
    def _():
      tile(a, j, q_chunk, s0_sc, s1_sc, masked, nxt)

    @pl.when(parity == 1)
    def _():
      tile(a, j, q_chunk, s1_sc, s0_sc, masked, nxt)

  for h in range(hp):
    score(0, 0, s0_sc, h)
  done = 0
  for a in range(qs):
    row0 = q_pos0 + (pl.program_id(2) * qs + a) * tq
    keys_all = jnp.clip(((row0 >> CHUNK_SHIFT) + 1) * CHUNK - k_pos0, 0, n_valid)
    keys_any = jnp.clip((((row0 + tq - 1) >> CHUNK_SHIFT) + 1) * CHUNK - k_pos0, 0, n_valid)
    last = (keys_any + tk - 1) // tk - 1
    n_open = jnp.minimum(keys_all // tk, last)
    q_chunk = (row0 + lax.broadcasted_iota(jnp.int32, (1, tq), 1)) >> CHUNK_SHIFT
    acc_sc[...] = jnp.zeros(acc_sc.shape, F32)
    for h in range(hp):
      ml_sc[2 * h] = jnp.full((1, tq), NEG_INF, F32)
      ml_sc[2 * h + 1] = jnp.zeros((1, tq), F32)

    def loop_body(j, carry, masked, a=a, done=done, q_chunk=q_chunk):
      step(a, j, done, q_chunk, masked, (a, j + 1))
      return carry

    lax.fori_loop(0, n_open, functools.partial(loop_body, masked=False), 0)
    lax.fori_loop(n_open, last, functools.partial(loop_body, masked=True), 0)
    step(a, last, done, q_chunk, True, (a + 1, 0) if a + 1 < qs else None)
    for h in range(hp):
      o = acc_sc[h] * (1.0 / ml_sc[2 * h + 1])
      o_ref[0, h * V_HEAD:(h + 1) * V_HEAD, a * tq:(a + 1) * tq] = o.astype(BF16)
    done = done + last + 1


def _attn(q_t, k, v_t, n_valid, q_pos0, k_pos0, hp, tq, tk, online):
  b, h, _, t = q_t.shape
  n_keys = k.shape[2]
  assert q_pos0 >= k_pos0 and n_keys % tk == 0 and t % tq == 0 and h % hp == 0
  qs = ATTN_Q_SUBTILES if (t // tq) % ATTN_Q_SUBTILES == 0 else 1
  return pl.pallas_call(
      functools.partial(_attn_kernel, hp=hp, tq=tq, qs=qs, tk=tk, n_valid=n_valid,
                        q_pos0=q_pos0, k_pos0=k_pos0, online=online),
      grid=(b, h // hp, t // (qs * tq)),
      in_specs=[pl.BlockSpec((1, hp, HEAD_PAD, qs * tq), lambda i, j, l: (i, j, 0, l)),
                pl.BlockSpec((1, hp, n_keys, HEAD_PAD), lambda i, j, l: (i, j, 0, 0)),
                pl.BlockSpec((1, hp, V_HEAD, n_keys), lambda i, j, l: (i, j, 0, 0))],
      out_specs=pl.BlockSpec((1, hp * V_HEAD, qs * tq), lambda i, j, l: (i, j, l)),
      out_shape=jax.ShapeDtypeStruct((b, h * V_HEAD, t), BF16),
      scratch_shapes=[pltpu.VMEM((hp, V_HEAD, tq), F32), pltpu.VMEM((2 * hp, 1, tq), F32),
                      pltpu.VMEM((hp, tk, tq), F32), pltpu.VMEM((hp, tk, tq), F32)],
      compiler_params=_params("parallel", "parallel", "arbitrary"),
      name="attn",
  )(q_t, k, v_t)


def _out_kernel(x_ref, att_ref, hs_ref, ga_ref, gb_ref, wpa_ref, wpr_ref, wo_ref, g2_ref,
                wg_ref, wu_ref, wd_ref, y_ref, *, ff_chunk, att_feature_major):
  if att_feature_major:
    o_att = lax.dot_general(att_ref[0], wpa_ref[...], (((0,), (0,)), ((), ())),
                            preferred_element_type=F32)
  else:
    o_att = _mm(att_ref[0], wpa_ref[...])
  o_rnn = _mm(hs_ref[0], wpr_ref[...])
  merged = ga_ref[0].astype(F32) * o_att + gb_ref[0].astype(F32) * o_rnn
  x1 = x_ref[0] + _mm(merged.astype(BF16), wo_ref[...])
  xn = _rms(x1, g2_ref[...]).astype(BF16)
  acc = x1
  for c0 in range(0, wg_ref.shape[1], ff_chunk):
    g = _mm(xn, wg_ref[:, c0:c0 + ff_chunk])
    u = _mm(xn, wu_ref[:, c0:c0 + ff_chunk])
    acc = acc + _mm((g * _sigmoid(g) * u).astype(BF16), wd_ref[c0:c0 + ff_chunk, :])
  y_ref[0] = acc


def _out(x, att, hs, ga, gb, wpa, wpr, wo, g2, wg, wu, wd, tm, att_feature_major):
  b, t, d = x.shape
  d_ff = wg.shape[1]
  ff_chunk = 2 * LANES if d_ff % (2 * LANES) == 0 else d_ff
  row = lambda n: pl.BlockSpec((1, tm, n), lambda i, j: (i, j, 0))
  consts = [wpa, wpr, wo, g2, wg, wu, wd]
  return pl.pallas_call(
      functools.partial(_out_kernel, ff_chunk=ff_chunk, att_feature_major=att_feature_major),
      grid=(b, t // tm),
      in_specs=[row(d),
                pl.BlockSpec((1, att.shape[1], tm), lambda i, j: (i, 0, j)) if att_feature_major
                else row(att.shape[2]),
                row(hs.shape[2]), row(d), row(d)]
      + [_const_spec(a.shape) for a in consts],
      out_specs=row(d),
      out_shape=jax.ShapeDtypeStruct((b, t, d), F32),
      compiler_params=_params("parallel", "parallel"),
      name="out",
  )(x, att, hs, ga, gb, *consts)


def _quad(a):
  x1, x2 = a[..., :HALF], a[..., HALF:]
  return jnp.concatenate([x1, x2, -x2, x1], axis=-1)


def _gain_row(g):
  g1, g2 = g[QK_NOPE:QK_NOPE + HALF], g[QK_NOPE + HALF:]
  return jnp.concatenate([g[:QK_NOPE], g1, g2, g2, g1])[None, :]


def _rope_table(pos0, n):
  inv_freq = jnp.exp(-math.log(ROPE_THETA) * jnp.arange(HALF, dtype=F32) / HALF)
  ang = (pos0 + jnp.arange(n, dtype=jnp.int32)).astype(F32)[:, None] * inv_freq[None, :]
  c, s = jnp.cos(ang), jnp.sin(ang)
  return jnp.concatenate([c, c, s, s], axis=-1)


def _prep_layer(l, norm1_g, w_in, q_norm_g, w_uq, kv_norm_g, w_ukv, qk_q_g, qk_k_g,
                conv_w, conv_b, w_rg_a, b_rg_a, w_rg_x, b_rg_x, lru_lambda,
                w_proj_attn, w_proj_rnn, w_out, norm2_g, w_ffn_gate, w_ffn_up, w_ffn_down):
  q_lora, kv_lora = q_norm_g.shape[1], kv_norm_g.shape[1]
  d_rnn, d_model = conv_b.shape[1], norm1_g.shape[1]
  o_kr = q_lora + kv_lora
  o_x = o_kr + QK_ROPE
  w = w_in[l]
  w_in4 = jnp.concatenate([w[:, :o_kr], _quad(w[:, o_kr:o_x]), w[:, o_x:]], axis=1).astype(BF16)
  wq_t = w_uq[l].T.astype(BF16)
  wkv = w_ukv[l].reshape(kv_lora, N_HEADS, QK_NOPE + V_HEAD)
  wk = wkv[..., :QK_NOPE].reshape(kv_lora, N_HEADS * QK_NOPE).astype(BF16)
  wv_t = wkv[..., QK_NOPE:].reshape(kv_lora, N_HEADS * V_HEAD).T.astype(BF16)
  w_gate = jnp.concatenate([w_rg_a[l], w_rg_x[l]], axis=-1).astype(BF16)
  shift = (SHIFT_MARGIN * QK_HEAD * SCALE * LOG2E) * (
      jnp.max(jnp.abs(qk_q_g[l])) * jnp.max(jnp.abs(qk_k_g[l])))
  row = lambda a: a[l][None, :]
  tile8 = lambda a: a[l].reshape(SUBLANES, LANES)
  return dict(
      widths=(q_lora, kv_lora, d_rnn, d_model),
      g1=row(norm1_g), w_in=w_in4, gq=row(q_norm_g), gkv=row(kv_norm_g),
      wq_t=wq_t, gq_t=jnp.broadcast_to(_gain_row(qk_q_g[l]).T, (HEAD_PAD, LANES)),
      wk=wk, wv_t=wv_t, gkh=_gain_row(qk_k_g[l]), shift=shift.reshape(1).astype(F32),
      cw=conv_w[l].reshape(CONV_W, SUBLANES, LANES), cb=tile8(conv_b), w_gate=w_gate,
      ba=tile8(b_rg_a), bx=tile8(b_rg_x), lam=tile8(lru_lambda),
      wpa=w_proj_attn[l].astype(BF16), wpr=w_proj_rnn[l].astype(BF16), wo=w_out[l].astype(BF16),
      g2=row(norm2_g), wg=w_ffn_gate[l].astype(BF16), wu=w_ffn_up[l].astype(BF16),
      wd=w_ffn_down[l].astype(BF16))


def _layer(x, pos0, past, p):
  b, t, _ = x.shape
  d_rnn = p["widths"][2]
  assert t >= SUBLANES and t % SUBLANES == 0 and d_rnn == SUBLANES * LANES
  if past is None:
    hist = jnp.zeros((b, HIST_ROWS, LANES), F32)
    h0 = jnp.zeros((b, SUBLANES, LANES), F32)
  else:
    past_ckv, past_krope, conv_hist, state_h = past
    hist = conv_hist.reshape(b, HIST_ROWS, LANES)
    h0 = state_h.reshape(b, SUBLANES, LANES)
  tm = _tile(t, ROW_TILE)
  cqn, ckv, kr4, kr, xtail, ga, gb, hs, hl = _in_rnn(x, hist, h0, p, tm)

  t_pad = -(-t // LANES) * LANES
  q_t = _q_up(p["shift"], _pad_axis(cqn, 1, t_pad), p["wq_t"], p["gq_t"],
              _rope_table(pos0, t_pad).T, _tile(t_pad, UP_TILE, LANES))
  if past is None:
    ckv_all, kr4_all, k_pos0 = ckv, kr4, pos0
  else:
    k_pos0 = pos0 - past_ckv.shape[1]
    ckv_all = jnp.concatenate([past_ckv, ckv], axis=1)
    kr4_all = jnp.concatenate([_quad(past_krope), kr4], axis=1)
  n_keys = ckv_all.shape[1]
  n_pad = -(-n_keys // LANES) * LANES
  k, v_t = _kv_up(_pad_axis(ckv_all, 1, n_pad), _pad_axis(kr4_all, 1, n_pad), p["wk"], p["wv_t"],
                  p["gkh"], _rope_table(k_pos0, n_pad), _tile(n_pad, UP_TILE, LANES))
  attend = functools.partial(
      _attn, n_valid=n_keys, q_pos0=pos0, k_pos0=k_pos0, hp=ATTN_HEADS_PER_STEP,
      tq=_tile(t_pad, ATTN_TQ, LANES), tk=_tile(n_pad, ATTN_TK, LANES))
  att = lax.cond(p["shift"][0] <= MAX_STATIC_SHIFT,
                 functools.partial(attend, online=False), functools.partial(attend, online=True),
                 q_t, k, v_t)[:, :, :t]

  row_inputs = (x, att, hs, ga, gb)
  fold = t < ROW_TILE
  if fold:
    row_inputs = (x, jnp.swapaxes(att, 1, 2), hs, ga, gb)
    row_inputs = tuple(a.reshape(1, b * t, a.shape[2]) for a in row_inputs)
  y = _out(*row_inputs, p["wpa"], p["wpr"], p["wo"], p["g2"], p["wg"], p["wu"], p["wd"],
           _tile(row_inputs[0].shape[1], ROW_TILE), att_feature_major=not fold).reshape(x.shape)
  return y, ckv, kr, xtail[:, SUBLANES - (CONV_W - 1):, :], hl.reshape(b, d_rnn)


def kernel(x_prompt, x_sample, cache_ckv, cache_krope, state_conv, state_h,
           norm1_g, w_in, q_norm_g, w_uq, kv_norm_g, w_ukv, qk_q_g, qk_k_g,
           conv_w, conv_b, w_rg_a, b_rg_a, w_rg_x, b_rg_x, lru_lambda,
           w_proj_attn, w_proj_rnn, w_out, norm2_g, w_ffn_gate, w_ffn_up, w_ffn_down):
  depth = w_in.shape[0]
  past_len = cache_ckv.shape[2]
  y_p, y_s = x_prompt, x_sample
  outs_p, outs_s = [], []
  for l in range(depth):
    p = _prep_layer(l, norm1_g, w_in, q_norm_g, w_uq, kv_norm_g, w_ukv, qk_q_g, qk_k_g,
                    conv_w, conv_b, w_rg_a, b_rg_a, w_rg_x, b_rg_x, lru_lambda,
                    w_proj_attn, w_proj_rnn, w_out, norm2_g, w_ffn_gate, w_ffn_up, w_ffn_down)
    y_p, *new_p = _layer(y_p, 0, None, p)
    past = (cache_ckv[l], cache_krope[l], state_conv[l], state_h[l])
    y_s, *new_s = _layer(y_s, past_len, past, p)
    outs_p.append(new_p)
    outs_s.append(new_s)
  stack = lambda outs, i: jnp.stack([o[i] for o in outs])
  return (y_p, y_s,
          stack(outs_p, 0), stack(outs_p, 1), stack(outs_p, 2), stack(outs_p, 3),
          stack(outs_s, 0), stack(outs_s, 1), stack(outs_s, 2), stack(outs_s, 3))
```
